```python
import jax, jax.numpy as jnp
from jax import lax
import numpy as np

D_MODEL = 2048
BATCH = 2
SEQ = 4096
DEPTH = 4
DEC_BATCH = 32
DEC_SEQ = 32
PAST_LEN = 2048

CHUNK = 64
N_META = 16
MIX_WIDTH = D_MODEL
HEAD_DIM = 128
ATTN_WIDTH = MIX_WIDTH // 2
CONV_WIDTH = MIX_WIDTH - ATTN_WIDTH
N_HEADS = ATTN_WIDTH // HEAD_DIM
CONV_GROUP = 128
CONV_K = 3
D_FF = ((8 * D_MODEL // 3 + 255) // 256) * 256
IN_WIDTH = 3 * ATTN_WIDTH + 3 * CONV_WIDTH
SPLITS = [ATTN_WIDTH, 2 * ATTN_WIDTH, 3 * ATTN_WIDTH, 3 * ATTN_WIDTH + CONV_WIDTH, 3 * ATTN_WIDTH + 2 * CONV_WIDTH]
Q_BLOCK = 128
EPS = 1e-6
SB_SCALE = HEAD_DIM ** -0.5

kernel_name = "hymba_stickbreak_shortconv_stream_step"


def rms_norm(x, g):
    xf = x.astype(jnp.float32)
    y = xf * lax.rsqrt(jnp.mean(xf * xf, axis=-1, keepdims=True) + EPS)
    return (y * g.astype(jnp.float32)).astype(x.dtype)


def group_rms_norm(x, g, group):
    shp = x.shape
    n_grp = shp[-1] // group
    y = rms_norm(x.reshape(shp[:-1] + (n_grp, group)), g.reshape(n_grp, group))
    return y.reshape(shp)


def sb_attend(q, k, v, q_pos, k_pos):
    z = jnp.einsum('bqhd,bkhd->bhqk', q, k).astype(jnp.float32) * SB_SCALE
    visible = (k_pos[None, :] < q_pos[:, None])[None, None]
    log_1m_beta = jnp.where(visible, jax.nn.log_sigmoid(-z), 0.0)
    after = lax.cumsum(log_1m_beta, axis=3, reverse=True) - log_1m_beta
    w = jnp.where(visible, jnp.exp(jax.nn.log_sigmoid(z) + after), 0.0)
    return jnp.einsum('bhqk,bkhd->bqhd', w.astype(v.dtype), v)


def sb_attention_prompt(q, k, v):
    b, n, h, dh = q.shape
    n_blk = -(-n // Q_BLOCK)
    pad = n_blk * Q_BLOCK - n
    padw = ((0, 0), (0, pad), (0, 0), (0, 0))
    qp, kp, vp = jnp.pad(q, padw), jnp.pad(k, padw), jnp.pad(v, padw)
    pos = jnp.arange(n_blk * Q_BLOCK)
    q_blocks = qp.reshape(b, n_blk, Q_BLOCK, h, dh).transpose(1, 0, 2, 3, 4)
    pos_blocks = pos.reshape(n_blk, Q_BLOCK)
    out = lax.map(lambda a: sb_attend(a[0], kp, vp, a[1], pos), (q_blocks, pos_blocks))
    return out.transpose(1, 0, 2, 3, 4).reshape(b, n_blk * Q_BLOCK, h, dh)[:, :n]


def sb_attention_cached(q, k, v, k_cache, v_cache):
    p, t = k_cache.shape[1], q.shape[1]
    keys = jnp.concatenate([k_cache.astype(k.dtype), k], axis=1)
    vals = jnp.concatenate([v_cache.astype(v.dtype), v], axis=1)
    return sb_attend(q, keys, vals, p + jnp.arange(t), jnp.arange(p + t))


def short_conv(u, state, w):
    t = u.shape[1]
    ext = jnp.concatenate([state.astype(u.dtype), u], axis=1)
    y = sum(ext[:, i:i + t] * w[i].astype(u.dtype) for i in range(CONV_K))
    return y, ext[:, -(CONV_K - 1):]


def layer(h, attend, conv_state, w_in, w_out, conv_w, g_pre_mix, g_post_mix, g_attn_out,
          g_conv_out, g_pre_ffn, g_post_ffn, w_gate, w_up, w_down):
    b, t, _ = h.shape
    xn = rms_norm(h, g_pre_mix)
    proj = jnp.einsum('btd,de->bte', xn, w_in)
    q, k, v, gate_b, gate_c, conv_in = jnp.split(proj, SPLITS, axis=-1)
    q = q.reshape(b, t, N_HEADS, HEAD_DIM)
    k = k.reshape(b, t, N_HEADS, HEAD_DIM)
    v = v.reshape(b, t, N_HEADS, HEAD_DIM)
    attn = attend(q, k, v).reshape(b, t, ATTN_WIDTH)
    conv_y, conv_last = short_conv(gate_c * conv_in, conv_state, conv_w)
    mix = jnp.concatenate([group_rms_norm(attn, g_attn_out, HEAD_DIM),
                           group_rms_norm(gate_b * conv_y, g_conv_out, CONV_GROUP)], axis=-1)
    h = h + rms_norm(jnp.einsum('bte,ed->btd', mix, w_out), g_post_mix)
    xf = rms_norm(h, g_pre_ffn)
    ff = jax.nn.silu(jnp.einsum('btd,df->btf', xf, w_gate)) * jnp.einsum('btd,df->btf', xf, w_up)
    h = h + rms_norm(jnp.einsum('btf,fd->btd', ff, w_down), g_post_ffn)
    return h, k, v, conv_last


def setup_inputs(seed: int = 0) -> dict:
    key = jax.random.key(seed)
    ks = jax.random.split(key, 20)
    f32 = jnp.float32

    def nrm(k, shape, scale):
        return jax.random.normal(k, shape, f32) * scale

    def gain(k, shape):
        return 1.0 + 0.05 * jax.random.normal(k, shape, f32)

    return {
        "x_prompt": nrm(ks[0], (BATCH, SEQ, D_MODEL), 1.0),
        "x_sample": nrm(ks[1], (DEC_BATCH, DEC_SEQ, D_MODEL), 1.0),
        "cache_k": nrm(ks[2], (DEPTH, DEC_BATCH, PAST_LEN, N_HEADS, HEAD_DIM), 1.0),
        "cache_v": nrm(ks[3], (DEPTH, DEC_BATCH, PAST_LEN, N_HEADS, HEAD_DIM), 1.0),
        "state_conv": nrm(ks[4], (DEPTH, DEC_BATCH, CONV_K - 1, CONV_WIDTH), 1.0),
        "meta": nrm(ks[5], (N_META, D_MODEL), 1.0),
        "w_in": nrm(ks[6], (DEPTH, D_MODEL, IN_WIDTH), D_MODEL ** -0.5),
        "w_out": nrm(ks[7], (DEPTH, MIX_WIDTH, D_MODEL), MIX_WIDTH ** -0.5),
        "conv_w": nrm(ks[8], (DEPTH, CONV_K, CONV_WIDTH), CONV_K ** -0.5),
        "g_pre_mix": gain(ks[9], (DEPTH, D_MODEL)),
        "g_post_mix": gain(ks[10], (DEPTH, D_MODEL)),
        "g_attn_out": gain(ks[11], (DEPTH, ATTN_WIDTH)),
        "g_conv_out": gain(ks[12], (DEPTH, CONV_WIDTH)),
        "g_pre_ffn": gain(ks[13], (DEPTH, D_MODEL)),
        "g_post_ffn": gain(ks[14], (DEPTH, D_MODEL)),
        "w_gate": nrm(ks[15], (DEPTH, D_MODEL, D_FF), D_MODEL ** -0.5),
        "w_up": nrm(ks[16], (DEPTH, D_MODEL, D_FF), D_MODEL ** -0.5),
        "w_down": nrm(ks[17], (DEPTH, D_FF, D_MODEL), D_FF ** -0.5),
    }


def reference(x_prompt, x_sample, cache_k, cache_v, state_conv, meta, w_in, w_out, conv_w,
              g_pre_mix, g_post_mix, g_attn_out, g_conv_out, g_pre_ffn, g_post_ffn,
              w_gate, w_up, w_down):
    b = x_prompt.shape[0]
    hp = jnp.concatenate([jnp.broadcast_to(meta.astype(x_prompt.dtype)[None], (b, N_META, D_MODEL)),
                          x_prompt], axis=1)
    hs = x_sample
    zero_conv = jnp.zeros((b, CONV_K - 1, CONV_WIDTH), hp.dtype)
    kp_l, vp_l, cp_l, ks_l, vs_l, cs_l = [], [], [], [], [], []
    for l in range(DEPTH):
        params = (w_in[l], w_out[l], conv_w[l], g_pre_mix[l], g_post_mix[l], g_attn_out[l],
                  g_conv_out[l], g_pre_ffn[l], g_post_ffn[l], w_gate[l], w_up[l], w_down[l])
        hp, kp, vp, cp = layer(hp, sb_attention_prompt, zero_conv, *params)
        kc, vc = cache_k[l], cache_v[l]
        hs, ks_, vs_, cs_ = layer(hs, lambda q, k, v: sb_attention_cached(q, k, v, kc, vc),
                                  state_conv[l], *params)
        kp_l.append(kp); vp_l.append(vp); cp_l.append(cp)
        ks_l.append(ks_); vs_l.append(vs_); cs_l.append(cs_)
    y_prompt = hp[:, N_META:]
    y_sample = hs
    new_k_prompt = jnp.stack(kp_l)
    new_v_prompt = jnp.stack(vp_l)
    new_conv_prompt = jnp.stack(cp_l)
    new_k_sample = jnp.stack(ks_l)
    new_v_sample = jnp.stack(vs_l)
    new_conv_sample = jnp.stack(cs_l)
    return (y_prompt, y_sample, new_k_prompt, new_v_prompt, new_conv_prompt,
            new_k_sample, new_v_sample, new_conv_sample)
```

```python
import functools

import jax
import jax.numpy as jnp
from jax import lax
from jax.experimental import pallas as pl
from jax.experimental.pallas import tpu as pltpu

HEAD_DIM = 128
CONV_K = 3
N_META_ROWS = 16
EPS = 1e-6
ATTN_TILE = 128
CACHE_TILE = 256
CONV_TILE = 256
ROW_TILE_TARGET = 600
FF_TILE = 512
VMEM_LIMIT = 56 * 1024 * 1024

_F32 = jnp.float32
_BF16 = jnp.bfloat16
_NT = (((1,), (1,)), ((), ()))


def _pick_tile(n, target, mult=16):
    best = None
    for t in range(mult, target + 1, mult):
        if n % t == 0:
            best = t
    assert best is not None, (n, target)
    return best


def _params(sem):
    return pltpu.CompilerParams(dimension_semantics=sem, vmem_limit_bytes=VMEM_LIMIT)


def _rms(x, g):
    ms = jnp.mean(x * x, axis=-1, keepdims=True)
    return x * lax.rsqrt(ms + EPS) * g


def _norm_kernel(h_ref, g_ref, o_ref):
    o_ref[...] = _rms(h_ref[...], g_ref[...]).astype(o_ref.dtype)


def _norm(h, g, tm):
    n, d = h.shape
    return pl.pallas_call(
        _norm_kernel,
        grid=(n // tm,),
        in_specs=[pl.BlockSpec((tm, d), lambda i: (i, 0)),
                  pl.BlockSpec((1, d), lambda i: (0, 0))],
        out_specs=pl.BlockSpec((tm, d), lambda i: (i, 0)),
        out_shape=jax.ShapeDtypeStruct((n, d), _BF16),
        compiler_params=_params(("arbitrary",)),
        name="pre_norm",
    )(h, g)


def _inproj_kernel(x_ref, w_ref, p_ref, pb_ref):
    acc = jnp.dot(x_ref[...], w_ref[...], preferred_element_type=_F32)
    p_ref[...] = acc

    @pl.when(pl.program_id(1) < 3)
    def _():
        pb_ref[...] = acc.astype(_BF16)


def _inproj(xn, w, tm, width):
    n, d = xn.shape
    return pl.pallas_call(
        _inproj_kernel,
        grid=(n // tm, 6),
        in_specs=[pl.BlockSpec((tm, d), lambda i, j: (i, 0)),
                  pl.BlockSpec((d, width), lambda i, j: (0, j))],
        out_specs=[pl.BlockSpec((tm, width), lambda i, j: (i, j)),
                   pl.BlockSpec((tm, width), lambda i, j: (i, jnp.minimum(j, 2)))],
        out_shape=[jax.ShapeDtypeStruct((n, 6 * width), _F32),
                   jax.ShapeDtypeStruct((n, 3 * width), _BF16)],
        compiler_params=_params(("arbitrary", "arbitrary")),
        name="in_proj",
    )(xn, w)


def _log_sigmoids(z):
    lsp = jnp.minimum(z, 0.0) - jnp.log(1.0 + jnp.exp(-jnp.abs(z)))
    return lsp, lsp - z


def _suffix_sums(l_mat, uo):
    hi = l_mat.astype(_BF16)
    lo = (l_mat - hi.astype(_F32)).astype(_BF16)
    return (jnp.dot(hi, uo, preferred_element_type=_F32)
            + jnp.dot(lo, uo, preferred_element_type=_F32))


def _suffix_matrix(rows, width):
    j = lax.broadcasted_iota(jnp.int32, (rows, 2 * width), 0)
    s = lax.broadcasted_iota(jnp.int32, (rows, 2 * width), 1)
    return jnp.where((s >= width) | (j > s), 1.0, 0.0).astype(_BF16)


def _head_norm_store(o_ref, acc_ref, g_ref, n_heads):
    for h in range(n_heads):
        hs = slice(h * HEAD_DIM, (h + 1) * HEAD_DIM)
        o_ref[:, hs] = _rms(acc_ref[:, hs], g_ref[:, hs]).astype(o_ref.dtype)


def _attn_prompt_kernel(q_ref, k_ref, v_ref, uo_ref, g_ref, o_ref, acc_ref, car_ref, *, n_heads, scale):
    tq = ATTN_TILE
    i = pl.program_id(1)
    acc_ref[...] = jnp.zeros_like(acc_ref)
    car_ref[...] = jnp.zeros_like(car_ref)
    uo = uo_ref[...]
    delta = (lax.broadcasted_iota(jnp.int32, (tq, tq), 1)
             - lax.broadcasted_iota(jnp.int32, (tq, tq), 0))

    def body(t, carry):
        k0 = pl.multiple_of((i - t) * tq, tq)
        vis = delta < t * tq
        for h in range(n_heads):
            hs = slice(h * HEAD_DIM, (h + 1) * HEAD_DIM)
            z = lax.dot_general(q_ref[:, hs], k_ref[pl.ds(k0, tq), hs], _NT,
                                preferred_element_type=_F32) * scale
            lsp, lsn = _log_sigmoids(z)
            l_mat = jnp.where(vis, lsn, 0.0)
            r = _suffix_sums(l_mat, uo)
            car = car_ref[:, hs]
            w = jnp.where(vis, jnp.exp(lsp + r[:, :tq] + car), 0.0)
            acc_ref[:, hs] += jnp.dot(w.astype(_BF16), v_ref[pl.ds(k0, tq), hs],
                                      preferred_element_type=_F32)
            car_ref[:, hs] = car + r[:, tq:]
        return carry

    lax.fori_loop(0, i + 1, body, 0)
    _head_norm_store(o_ref, acc_ref, g_ref, n_heads)


def _attn_prompt(pb, g_attn, n_tok, batch, tp, width, scale):
    tq = ATTN_TILE
    nq = tp // tq
    n_heads = width // HEAD_DIM
    uo = _suffix_matrix(tq, tq)
    return pl.pallas_call(
        functools.partial(_attn_prompt_kernel, n_heads=n_heads, scale=scale),
        grid=(batch, nq),
        in_specs=[pl.BlockSpec((tq, width), lambda b, i: (b * nq + i, 0)),
                  pl.BlockSpec((tp, width), lambda b, i: (b, 1)),
                  pl.BlockSpec((tp, width), lambda b, i: (b, 2)),
                  pl.BlockSpec((tq, 2 * tq), lambda b, i: (0, 0)),
                  pl.BlockSpec((1, width), lambda b, i: (0, 0))],
        out_specs=pl.BlockSpec((tq, width), lambda b, i: (b * nq + i, 0)),
        out_shape=jax.ShapeDtypeStruct((n_tok, width), _BF16),
        scratch_shapes=[pltpu.VMEM((tq, width), _F32), pltpu.VMEM((tq, width), _F32)],
        compiler_params=_params(("arbitrary", "arbitrary")),
        name="attn_prompt",
    )(pb, pb, pb, uo, g_attn)


def _attn_sample_kernel(q_ref, kn_ref, vn_ref, ck_ref, cv_ref, uos_ref, uoc_ref, g_ref, buf_ref,
                        o_ref, acc_ref, car_ref, *, n_heads, scale, ds):
    del buf_ref
    tk = CACHE_TILE
    s = pl.program_id(1)

    @pl.when(s == 0)
    def _():
        vis = (lax.broadcasted_iota(jnp.int32, (ds, ds), 1)
               < lax.broadcasted_iota(jnp.int32, (ds, ds), 0))
        uo = uos_ref[...]
        for h in range(n_heads):
            hs = slice(h * HEAD_DIM, (h + 1) * HEAD_DIM)
            z = lax.dot_general(q_ref[:, hs], kn_ref[:, hs], _NT, preferred_element_type=_F32) * scale
            lsp, lsn = _log_sigmoids(z)
            l_mat = jnp.where(vis, lsn, 0.0)
            r = _suffix_sums(l_mat, uo)
            w = jnp.where(vis, jnp.exp(lsp + r[:, :ds]), 0.0)
            acc_ref[:, hs] = jnp.dot(w.astype(_BF16), vn_ref[:, hs], preferred_element_type=_F32)
            car_ref[h] = r[:, tk:]

    @pl.when(s > 0)
    def _():
        uo = uoc_ref[...]
        for h in range(n_heads):
            hs = slice(h * HEAD_DIM, (h + 1) * HEAD_DIM)
            z = lax.dot_general(q_ref[:, hs], ck_ref[0, :, hs].astype(_BF16), _NT,
                                preferred_element_type=_F32) * scale
            lsp, lsn = _log_sigmoids(z)
            r = _suffix_sums(lsn, uo)
            car = car_ref[h]
            w = jnp.exp(lsp + r[:, :tk] + car)
            acc_ref[:, hs] += jnp.dot(w.astype(_BF16), cv_ref[0, :, hs].astype(_BF16),
                                      preferred_element_type=_F32)
            car_ref[h] = car + r[:, tk:]

    @pl.when(s == pl.num_programs(1) - 1)
    def _():
        _head_norm_store(o_ref, acc_ref, g_ref, n_heads)


def _attn_sample(pb, cache_k, cache_v, g_attn, attn_buf, row0, n_streams, ds, width, scale):
    tk = CACHE_TILE
    past = cache_k.shape[1]
    n_ct = past // tk
    n_heads = width // HEAD_DIM
    blk0 = row0 // ds
    uos = _suffix_matrix(ds, tk)
    uoc = _suffix_matrix(tk, tk)

    def cache_map(b, s):
        return (b, n_ct - jnp.maximum(s, 1), 0)

    return pl.pallas_call(
        functools.partial(_attn_sample_kernel, n_heads=n_heads, scale=scale, ds=ds),
        grid=(n_streams, n_ct + 1),
        in_specs=[pl.BlockSpec((ds, width), lambda b, s: (blk0 + b, 0)),
                  pl.BlockSpec((ds, width), lambda b, s: (blk0 + b, 1)),
                  pl.BlockSpec((ds, width), lambda b, s: (blk0 + b, 2)),
                  pl.BlockSpec((1, tk, width), cache_map),
                  pl.BlockSpec((1, tk, width), cache_map),
                  pl.BlockSpec((ds, 2 * tk), lambda b, s: (0, 0)),
                  pl.BlockSpec((tk, 2 * tk), lambda b, s: (0, 0)),
                  pl.BlockSpec((1, width), lambda b, s: (0, 0)),
                  pl.BlockSpec(memory_space=pl.ANY)],
        out_specs=pl.BlockSpec((ds, width), lambda b, s: (blk0 + b, 0)),
        out_shape=jax.ShapeDtypeStruct(attn_buf.shape, attn_buf.dtype),
        scratch_shapes=[pltpu.VMEM((ds, width), _F32), pltpu.VMEM((n_heads, ds, tk), _F32)],
        input_output_aliases={8: 0},
        compiler_params=_params(("arbitrary", "arbitrary")),
        name="attn_sample",
    )(pb, pb, pb, cache_k, cache_v, uos, uoc, g_attn, attn_buf)


def _conv_kernel(gb_ref, gc_ref, ci_ref, s0_ref, s1_ref, w_ref, g_ref, o_ref, u_ref, ubuf_ref,
                 *, tp, n_prompt, ds, width):
    tc = CONV_TILE
    i = pl.program_id(0)

    @pl.when(i == 0)
    def _():
        ubuf_ref[0:8, :] = jnp.zeros((8, width), _F32)

    u = gc_ref[...] * ci_ref[...]
    u_ref[...] = u
    ubuf_ref[8:8 + tc, :] = u
    u1 = ubuf_ref[7:7 + tc, :]
    u2 = ubuf_ref[6:6 + tc, :]

    row = i * tc + lax.broadcasted_iota(jnp.int32, (tc, 1), 0)
    is_sample = row >= n_prompt
    pos = jnp.where(is_sample, lax.rem(row - n_prompt, ds), lax.rem(row, tp))
    first = pos == 0
    second = pos == 1
    seqs = tc // ds
    zero = jnp.zeros((tc, width), _F32)
    st0 = jnp.broadcast_to(s0_ref[...][:, None, :], (seqs, ds, width)).reshape(tc, width)
    st1 = jnp.broadcast_to(s1_ref[...][:, None, :], (seqs, ds, width)).reshape(tc, width)
    st0 = jnp.where(is_sample, st0, zero)
    st1 = jnp.where(is_sample, st1, zero)
    u1 = jnp.where(first, st1, u1)
    u2 = jnp.where(first, st0, jnp.where(second, st1, u2))
    y = u2 * w_ref[0:1, :] + u1 * w_ref[1:2, :] + u * w_ref[2:3, :]
    c = gb_ref[...] * y
    for grp in range(width // HEAD_DIM):
        gs = slice(grp * HEAD_DIM, (grp + 1) * HEAD_DIM)
        o_ref[:, gs] = _rms(c[:, gs], g_ref[:, gs]).astype(o_ref.dtype)
    ubuf_ref[0:8, :] = u[tc - 8:tc, :]


def _conv(p, s0, s1, conv_w, g_conv, tp, n_prompt, ds, width):
    tc = CONV_TILE
    n = p.shape[0]
    seqs = tc // ds
    first_sample_tile = n_prompt // tc

    def state_map(i):
        return (jnp.maximum(i - first_sample_tile, 0), 0)

    return pl.pallas_call(
        functools.partial(_conv_kernel, tp=tp, n_prompt=n_prompt, ds=ds, width=width),
        grid=(n // tc,),
        in_specs=[pl.BlockSpec((tc, width), lambda i: (i, 3)),
                  pl.BlockSpec((tc, width), lambda i: (i, 4)),
                  pl.BlockSpec((tc, width), lambda i: (i, 5)),
                  pl.BlockSpec((seqs, width), state_map),
                  pl.BlockSpec((seqs, width), state_map),
                  pl.BlockSpec((CONV_K, width), lambda i: (0, 0)),
                  pl.BlockSpec((1, width), lambda i: (0, 0))],
        out_specs=[pl.BlockSpec((tc, width), lambda i: (i, 0)),
                   pl.BlockSpec((tc, width), lambda i: (i, 0))],
        out_shape=[jax.ShapeDtypeStruct((n, width), _BF16),
                   jax.ShapeDtypeStruct((n, width), _F32)],
        scratch_shapes=[pltpu.VMEM((8 + tc, width), _F32)],
        compiler_params=_params(("arbitrary",)),
        name="gated_conv",
    )(p, p, p, s0, s1, conv_w, g_conv)


def _outproj_kernel(a_ref, c_ref, w_ref, h_ref, gpost_ref, gnext_ref, ho_ref, xo_ref, *, width):
    y = (jnp.dot(a_ref[...], w_ref[0:width, :], preferred_element_type=_F32)
         + jnp.dot(c_ref[...], w_ref[width:2 * width, :], preferred_element_type=_F32))
    h_new = h_ref[...] + _rms(y, gpost_ref[...])
    ho_ref[...] = h_new
    xo_ref[...] = _rms(h_new, gnext_ref[...]).astype(xo_ref.dtype)


def _outproj(attn_n, conv_n, w_out, h, g_post, g_next, tm):
    n, d = h.shape
    width = attn_n.shape[1]
    return pl.pallas_call(
        functools.partial(_outproj_kernel, width=width),
        grid=(n // tm,),
        in_specs=[pl.BlockSpec((tm, width), lambda i: (i, 0)),
                  pl.BlockSpec((tm, width), lambda i: (i, 0)),
                  pl.BlockSpec((2 * width, d), lambda i: (0, 0)),
                  pl.BlockSpec((tm, d), lambda i: (i, 0)),
                  pl.BlockSpec((1, d), lambda i: (0, 0)),
                  pl.BlockSpec((1, d), lambda i: (0, 0))],
        out_specs=[pl.BlockSpec((tm, d), lambda i: (i, 0)),
                   pl.BlockSpec((tm, d), lambda i: (i, 0))],
        out_shape=[jax.ShapeDtypeStruct((n, d), _F32),
                   jax.ShapeDtypeStruct((n, d), _BF16)],
        compiler_params=_params(("arbitrary",)),
        name="out_proj",
    )(attn_n, conv_n, w_out, h, g_post, g_next)


def _ffn_kernel(x_ref, wg_ref, wu_ref, wd_ref, h_ref, gpost_ref, gnext_ref, ho_ref, xo_ref):
    c = pl.program_id(1)
    x = x_ref[...]
    gate = jnp.dot(x, wg_ref[...], preferred_element_type=_F32)
    up = jnp.dot(x, wu_ref[...], preferred_element_type=_F32)
    act = (gate * jax.nn.sigmoid(gate) * up).astype(_BF16)
    part = jnp.dot(act, wd_ref[...], preferred_element_type=_F32)

    @pl.when(c == 0)
    def _():
        ho_ref[...] = part

    @pl.when(c > 0)
    def _():
        ho_ref[...] += part

    @pl.when(c == pl.num_programs(1) - 1)
    def _():
        h_new = h_ref[...] + _rms(ho_ref[...], gpost_ref[...])
        ho_ref[...] = h_new
        xo_ref[...] = _rms(h_new, gnext_ref[...]).astype(xo_ref.dtype)


def _ffn(xf, w_gate, w_up, w_down, h, g_post, g_next, tm):
    n, d = h.shape
    d_ff = w_gate.shape[1]
    tf = FF_TILE
    return pl.pallas_call(
        _ffn_kernel,
        grid=(n // tm, d_ff // tf),
        in_specs=[pl.BlockSpec((tm, d), lambda i, c: (i, 0)),
                  pl.BlockSpec((d, tf), lambda i, c: (0, c)),
                  pl.BlockSpec((d, tf), lambda i, c: (0, c)),
                  pl.BlockSpec((tf, d), lambda i, c: (c, 0)),
                  pl.BlockSpec((tm, d), lambda i, c: (i, 0)),
                  pl.BlockSpec((1, d), lambda i, c: (0, 0)),
                  pl.BlockSpec((1, d), lambda i, c: (0, 0))],
        out_specs=[pl.BlockSpec((tm, d), lambda i, c: (i, 0)),
                   pl.BlockSpec((tm, d), lambda i, c: (i, 0))],
        out_shape=[jax.ShapeDtypeStruct((n, d), _F32),
                   jax.ShapeDtypeStruct((n, d), _BF16)],
        compiler_params=_params(("arbitrary", "arbitrary")),
        name="swiglu_ffn",
    )(xf, w_gate, w_up, w_down, h, g_post, g_next)


def kernel(x_prompt, x_sample, cache_k, cache_v, state_conv, meta, w_in, w_out, conv_w, g_pre_mix, g_post_mix,
           g_attn_out, g_conv_out, g_pre_ffn, g_post_ffn, w_gate, w_up, w_down):
    batch, seq, d = x_prompt.shape
    n_streams, ds, _ = x_sample.shape
    depth = w_in.shape[0]
    width = w_out.shape[1] // 2
    n_heads = width // HEAD_DIM
    n_meta = meta.shape[0]
    past = cache_k.shape[2]
    t_real = n_meta + seq
    tp = -(-t_real // ATTN_TILE) * ATTN_TILE
    n_prompt = batch * tp
    n_sample = n_streams * ds
    n_tok = n_prompt + n_sample
    assert n_prompt % CONV_TILE == 0 and n_sample % CONV_TILE == 0 and CONV_TILE % ds == 0
    assert tp % ds == 0 and past % CACHE_TILE == 0 and ds >= CONV_K - 1
    assert w_gate.shape[2] % FF_TILE == 0
    tm = _pick_tile(n_tok, ROW_TILE_TARGET)
    scale = HEAD_DIM ** -0.5

    pad = jnp.zeros((batch, tp - t_real, d), _F32)
    hp = jnp.concatenate([jnp.broadcast_to(meta.astype(_F32)[None], (batch, n_meta, d)), x_prompt, pad], axis=1)
    h = jnp.concatenate([hp.reshape(n_prompt, d), x_sample.reshape(n_sample, d)], axis=0)

    bf = lambda a: a.astype(_BF16)
    w_in_b, w_out_b, w_gate_b, w_up_b, w_down_b = bf(w_in), bf(w_out), bf(w_gate), bf(w_up), bf(w_down)
    row = lambda a: a.reshape(1, -1)
    cache_k2 = cache_k.reshape(depth, n_streams, past, width)
    cache_v2 = cache_v.reshape(depth, n_streams, past, width)

    xn = _norm(h, row(g_pre_mix[0]), tm)
    kp, vp, cp, ks, vs, cs = [], [], [], [], [], []
    for l in range(depth):
        p, pb = _inproj(xn, w_in_b[l], tm, width)
        attn_n = _attn_prompt(pb, row(g_attn_out[l]), n_tok, batch, tp, width, scale)
        attn_n = _attn_sample(pb, cache_k2[l], cache_v2[l], row(g_attn_out[l]), attn_n,
                              n_prompt, n_streams, ds, width, scale)
        conv_n, u = _conv(p, state_conv[l, :, 0, :], state_conv[l, :, 1, :], conv_w[l],
                          row(g_conv_out[l]), tp, n_prompt, ds, width)
        h, xf = _outproj(attn_n, conv_n, w_out_b[l], h, row(g_post_mix[l]), row(g_pre_ffn[l]), tm)
        g_next = g_pre_mix[min(l + 1, depth - 1)]
        h, xn = _ffn(xf, w_gate_b[l], w_up_b[l], w_down_b[l], h, row(g_post_ffn[l]), row(g_next), tm)

        k_all = p[:, width:2 * width]
        v_all = p[:, 2 * width:3 * width]
        kp.append(k_all[:n_prompt].reshape(batch, tp, n_heads, HEAD_DIM)[:, :t_real])
        vp.append(v_all[:n_prompt].reshape(batch, tp, n_heads, HEAD_DIM)[:, :t_real])
        cp.append(u[:n_prompt].reshape(batch, tp, width)[:, t_real - (CONV_K - 1):t_real])
        ks.append(k_all[n_prompt:].reshape(n_streams, ds, n_heads, HEAD_DIM))
        vs.append(v_all[n_prompt:].reshape(n_streams, ds, n_heads, HEAD_DIM))
        cs.append(u[n_prompt:].reshape(n_streams, ds, width)[:, ds - (CONV_K - 1):])

    y_prompt = h[:n_prompt].reshape(batch, tp, d)[:, n_meta:t_real]
    y_sample = h[n_prompt:].reshape(n_streams, ds, d)
    return (y_prompt, y_sample, jnp.stack(kp), jnp.stack(vp), jnp.stack(cp),
            jnp.stack(ks), jnp.stack(vs), jnp.stack(cs))
```

```python
import functools

import jax
import jax.numpy as jnp
from jax import lax
from jax.experimental import pallas as pl
from jax.experimental.pallas import tpu as pltpu

HEAD_DIM = 128
CONV_K = 3
EPS = 1e-6
ATTN_TILE = 128
CACHE_TILE = 256
CONV_TILE = 256
ROW_TILE_TARGET = 600
FF_TILE = 512
VMEM_LIMIT = 56 * 1024 * 1024
LOG_WEIGHT_FLOOR = -105.0

_F32 = jnp.float32
_BF16 = jnp.bfloat16
_NT = (((1,), (1,)), ((), ()))


def _pick_tile(n, target, mult=16):
    best = None
    for t in range(mult, target + 1, mult):
        if n % t == 0:
            best = t
    assert best is not None, (n, target)
    return best


def _params(sem):
    return pltpu.CompilerParams(dimension_semantics=sem, vmem_limit_bytes=VMEM_LIMIT)


def _rms(x, g):
    ms = jnp.mean(x * x, axis=-1, keepdims=True)
    return x * lax.rsqrt(ms + EPS) * g


def _head_slices(n_heads):
    return [slice(h * HEAD_DIM, (h + 1) * HEAD_DIM) for h in range(n_heads)]


def _norm_kernel(h_ref, g_ref, o_ref):
    o_ref[...] = _rms(h_ref[...], g_ref[...]).astype(o_ref.dtype)


def _norm(h, g, layer, tm):
    n, d = h.shape
    return pl.pallas_call(
        _norm_kernel,
        grid=(n // tm,),
        in_specs=[pl.BlockSpec((tm, d), lambda i: (i, 0)),
                  pl.BlockSpec((None, 1, d), lambda i: (layer, 0, 0))],
        out_specs=pl.BlockSpec((tm, d), lambda i: (i, 0)),
        out_shape=jax.ShapeDtypeStruct((n, d), _BF16),
        compiler_params=_params(("arbitrary",)),
        name="pre_norm",
    )(h, g)


def _inproj_kernel(x_ref, w_ref, p_ref, pb_ref, k_ref, v_ref, *, n_heads, tm):
    j = pl.program_id(1)
    acc = jnp.dot(x_ref[...], w_ref[...], preferred_element_type=_F32)

    @pl.when(j < 3)
    def _():
        pb_ref[...] = acc.astype(_BF16)

    @pl.when(j >= 3)
    def _():
        p_ref[...] = acc

    def store_heads(ref):
        for h, hs in enumerate(_head_slices(n_heads)):
            ref[pl.ds(h, tm, stride=n_heads), :] = acc[:, hs]

    @pl.when(j == 1)
    def _():
        store_heads(k_ref)

    @pl.when(j == 2)
    def _():
        store_heads(v_ref)


def _inproj(xn, w, layer, tm, width):
    n, d = xn.shape
    n_heads = width // HEAD_DIM
    kv_shape = jax.ShapeDtypeStruct((n * n_heads, HEAD_DIM), _F32)
    kv_spec = pl.BlockSpec((tm * n_heads, HEAD_DIM), lambda i, j: (i, 0))
    return pl.pallas_call(
        functools.partial(_inproj_kernel, n_heads=n_heads, tm=tm),
        grid=(n // tm, 6),
        in_specs=[pl.BlockSpec((tm, d), lambda i, j: (i, 0)),
                  pl.BlockSpec((None, d, width), lambda i, j: (layer, 0, j))],
        out_specs=[pl.BlockSpec((tm, width), lambda i, j: (i, jnp.maximum(j - 3, 0))),
                   pl.BlockSpec((tm, width), lambda i, j: (i, jnp.minimum(j, 2))),
                   kv_spec, kv_spec],
        out_shape=[jax.ShapeDtypeStruct((n, 3 * width), _F32),
                   jax.ShapeDtypeStruct((n, 3 * width), _BF16),
                   kv_shape, kv_shape],
        compiler_params=_params(("arbitrary", "arbitrary")),
        name="in_proj",
    )(xn, w)


def _log_sigmoids(z):
    lsp = jnp.minimum(z, 0.0) - jnp.log(1.0 + jnp.exp(-jnp.abs(z)))
    return lsp, lsp - z


def _suffix_sums(l_mat, uo):
    hi = l_mat.astype(_BF16)
    lo = (l_mat - hi.astype(_F32)).astype(_BF16)
    if uo.shape[0] == 2 * l_mat.shape[1]:
        return jnp.dot(jnp.concatenate([hi, lo], axis=1), uo, preferred_element_type=_F32)
    return (jnp.dot(hi, uo, preferred_element_type=_F32)
            + jnp.dot(lo, uo, preferred_element_type=_F32))


def _suffix_matrix(rows, width, copies=1):
    j = lax.rem(lax.broadcasted_iota(jnp.int32, (copies * rows, 2 * width), 0), rows)
    s = lax.broadcasted_iota(jnp.int32, (copies * rows, 2 * width), 1)
    return jnp.where((s >= width) | (j > s), 1.0, 0.0).astype(_BF16)


def _sb_tile(qs, ks, vs, cars, uo, vis, scale):
    tk = ks[0].shape[0]
    wc = uo.shape[1] // 2
    zs = [lax.dot_general(q, k, _NT, preferred_element_type=_F32) * scale for q, k in zip(qs, ks)]
    lsps, rs = [], []
    for z in zs:
        lsp, lsn = _log_sigmoids(z)
        if vis is not None:
            lsn = jnp.where(vis, lsn, 0.0)
        lsps.append(lsp)
        rs.append(_suffix_sums(lsn, uo))
    pvs, new_cars = [], []
    for lsp, r, v, car in zip(lsps, rs, vs, cars):
        x = lsp + r[:, :tk]
        if car is not None:
            x = x + car[:, :tk]
        w = jnp.exp(x)
        if vis is not None:
            w = jnp.where(vis, w, 0.0)
        pvs.append(jnp.dot(w.astype(_BF16), v, preferred_element_type=_F32))
        new_cars.append(r[:, wc:] if car is None else car + r[:, wc:])
    return pvs, new_cars


def _head_norm_store(o_ref, acc_ref, g_ref, n_heads):
    for hs in _head_slices(n_heads):
        o_ref[:, hs] = _rms(acc_ref[:, hs], g_ref[:, hs]).astype(o_ref.dtype)


def _attn_prompt_kernel(q_ref, k_ref, v_ref, uo_ref, g_ref, o_ref, acc_ref, car_ref, *, n_heads, scale):
    tq = ATTN_TILE
    i = pl.program_id(1)
    uo = uo_ref[...]
    heads = _head_slices(n_heads)

    def tile(t, vis, first):
        k0 = pl.multiple_of((i - t) * tq, tq)
        qs = [q_ref[:, hs] for hs in heads]
        ks = [k_ref[pl.ds(k0, tq), hs] for hs in heads]
        vs = [v_ref[pl.ds(k0, tq), hs] for hs in heads]
        cars = [None if first else car_ref[:, hs] for hs in heads]
        pvs, new_cars = _sb_tile(qs, ks, vs, cars, uo, vis, scale)
        for hs, pv, car in zip(heads, pvs, new_cars):
            if first:
                acc_ref[:, hs] = pv
            else:
                acc_ref[:, hs] += pv
            car_ref[:, hs] = car

    causal = (lax.broadcasted_iota(jnp.int32, (tq, tq), 1)
              < lax.broadcasted_iota(jnp.int32, (tq, tq), 0))
    tile(0, causal, True)

    def cond(state):
        t, done = state
        return jnp.logical_and(t <= i, done == 0)

    def body(state):
        t, _ = state
        tile(t, None, False)
        done = jnp.max(car_ref[...]) < LOG_WEIGHT_FLOOR
        return t + 1, done.astype(jnp.int32)

    lax.while_loop(cond, body, (jnp.int32(1), jnp.int32(0)))
    _head_norm_store(o_ref, acc_ref, g_ref, n_heads)


def _attn_prompt(pb, g_attn, layer, batch, tp, width, scale):
    tq = ATTN_TILE
    nq = tp // tq
    n_heads = width // HEAD_DIM
    uo = _suffix_matrix(tq, tq, copies=2)
    return pl.pallas_call(
        functools.partial(_attn_prompt_kernel, n_heads=n_heads, scale=scale),
        grid=(batch, nq),
        in_specs=[pl.BlockSpec((tq, width), lambda b, i: (b * nq + i, 0)),
                  pl.BlockSpec((tp, width), lambda b, i: (b, 1)),
                  pl.BlockSpec((tp, width), lambda b, i: (b, 2)),
                  pl.BlockSpec((2 * tq, 2 * tq), lambda b, i: (0, 0)),
                  pl.BlockSpec((None, 1, width), lambda b, i: (layer, 0, 0))],
        out_specs=pl.BlockSpec((tq, width), lambda b, i: (b * nq + i, 0)),
        out_shape=jax.ShapeDtypeStruct((batch * tp, width), _BF16),
        scratch_shapes=[pltpu.VMEM((tq, width), _F32), pltpu.VMEM((tq, width), _F32)],
        compiler_params=_params(("arbitrary", "arbitrary")),
        name="attn_prompt",
    )(pb, pb, pb, uo, g_attn)


def _attn_sample_kernel(q_ref, kn_ref, vn_ref, ck_hbm, cv_hbm, uos_ref, uoc_ref, g_ref,
                        o_ref, acc_ref, car_ref, kbuf, vbuf, sem, *, n_heads, scale, ds, past, stream0):
    tk = CACHE_TILE
    n_ct = past // tk
    rows = tk * n_heads
    b = pl.program_id(0)
    heads = _head_slices(n_heads)

    def copies(blk, slot):
        row0 = pl.multiple_of(((stream0 + b) * past + blk * tk) * n_heads, rows)
        return (pltpu.make_async_copy(ck_hbm.at[pl.ds(row0, rows), :], kbuf.at[slot], sem.at[0, slot]),
                pltpu.make_async_copy(cv_hbm.at[pl.ds(row0, rows), :], vbuf.at[slot], sem.at[1, slot]))

    def start(blk, slot):
        for c in copies(blk, slot):
            c.start()

    def wait(blk, slot):
        for c in copies(blk, slot):
            c.wait()

    start(n_ct - 1, (n_ct - 1) % 2)

    causal = (lax.broadcasted_iota(jnp.int32, (ds, ds), 1)
              < lax.broadcasted_iota(jnp.int32, (ds, ds), 0))
    qs = [q_ref[:, hs] for hs in heads]
    pvs, cars = _sb_tile(qs, [kn_ref[:, hs] for hs in heads], [vn_ref[:, hs] for hs in heads],
                         [None] * n_heads, uos_ref[...], causal, scale)
    for h, hs in enumerate(heads):
        acc_ref[:, hs] = pvs[h]
        car_ref[h] = cars[h]

    def cond(state):
        blk, done = state
        return jnp.logical_and(blk >= 0, done == 0)

    def body(state):
        blk, _ = state
        slot = lax.rem(blk, 2)
        wait(blk, slot)

        @pl.when(blk > 0)
        def _():
            start(blk - 1, 1 - slot)

        kslot, vslot = kbuf.at[slot], vbuf.at[slot]
        ks = [kslot[pl.ds(h, tk, stride=n_heads), :].astype(_BF16) for h in range(n_heads)]
        vs = [vslot[pl.ds(h, tk, stride=n_heads), :].astype(_BF16) for h in range(n_heads)]
        pvs, cars = _sb_tile([q_ref[:, hs] for hs in heads], ks, vs,
                             [car_ref[h] for h in range(n_heads)], uoc_ref[...], None, scale)
        for h, hs in enumerate(heads):
            acc_ref[:, hs] += pvs[h]
            car_ref[h] = cars[h]
        done = jnp.max(car_ref[...]) < LOG_WEIGHT_FLOOR
        return blk - 1, done.astype(jnp.int32)

    blk_end, _ = lax.while_loop(cond, body, (jnp.int32(n_ct - 1), jnp.int32(0)))

    @pl.when(blk_end >= 0)
    def _():
        wait(blk_end, lax.rem(blk_end, 2))

    _head_norm_store(o_ref, acc_ref, g_ref, n_heads)


def _attn_sample(pb, cache_k, cache_v, g_attn, layer, row0, n_streams, ds, past, width, scale):
    tk = CACHE_TILE
    n_heads = width // HEAD_DIM
    blk0 = row0 // ds
    uos = _suffix_matrix(ds, tk)
    uoc = _suffix_matrix(tk, tk)
    return pl.pallas_call(
        functools.partial(_attn_sample_kernel, n_heads=n_heads, scale=scale, ds=ds, past=past,
                          stream0=layer * n_streams),
        grid=(n_streams,),
        in_specs=[pl.BlockSpec((ds, width), lambda b: (blk0 + b, 0)),
                  pl.BlockSpec((ds, width), lambda b: (blk0 + b, 1)),
                  pl.BlockSpec((ds, width), lambda b: (blk0 + b, 2)),
                  pl.BlockSpec(memory_space=pl.ANY),
                  pl.BlockSpec(memory_space=pl.ANY),
                  pl.BlockSpec((ds, 2 * tk), lambda b: (0, 0)),
                  pl.BlockSpec((tk, 2 * tk), lambda b: (0, 0)),
                  pl.BlockSpec((None, 1, width), lambda b: (layer, 0, 0))],
        out_specs=pl.BlockSpec((ds, width), lambda b: (b, 0)),
        out_shape=jax.ShapeDtypeStruct((n_streams * ds, width), _BF16),
        scratch_shapes=[pltpu.VMEM((ds, width), _F32),
                        pltpu.VMEM((n_heads, ds, tk), _F32),
                        pltpu.VMEM((2, tk * n_heads, HEAD_DIM), _F32),
                        pltpu.VMEM((2, tk * n_heads, HEAD_DIM), _F32),
                        pltpu.SemaphoreType.DMA((2, 2))],
        compiler_params=_params(("arbitrary",)),
        name="attn_sample",
    )(pb, pb, pb, cache_k, cache_v, uos, uoc, g_attn)


def _conv_kernel(gb_ref, gc_ref, ci_ref, s0_ref, s1_ref, w_ref, g_ref, o_ref, u_ref, ubuf_ref,
                 *, tp, n_prompt, ds, width):
    tc = CONV_TILE
    i = pl.program_id(0)

    @pl.when(i == 0)
    def _():
        ubuf_ref[0:8, :] = jnp.zeros((8, width), _F32)

    u = gc_ref[...] * ci_ref[...]
    u_ref[...] = u
    ubuf_ref[8:8 + tc, :] = u
    u1 = ubuf_ref[7:7 + tc, :]
    u2 = ubuf_ref[6:6 + tc, :]

    row = i * tc + lax.broadcasted_iota(jnp.int32, (tc, 1), 0)
    is_sample = row >= n_prompt
    pos = jnp.where(is_sample, lax.rem(row - n_prompt, ds), lax.rem(row, tp))
    first = pos == 0
    second = pos == 1
    seqs = tc // ds
    zero = jnp.zeros((tc, width), _F32)
    st0 = jnp.broadcast_to(s0_ref[...][:, None, :], (seqs, ds, width)).reshape(tc, width)
    st1 = jnp.broadcast_to(s1_ref[...][:, None, :], (seqs, ds, width)).reshape(tc, width)
    st0 = jnp.where(is_sample, st0, zero)
    st1 = jnp.where(is_sample, st1, zero)
    u1 = jnp.where(first, st1, u1)
    u2 = jnp.where(first, st0, jnp.where(second, st1, u2))
    y = u2 * w_ref[0:1, :] + u1 * w_ref[1:2, :] + u * w_ref[2:3, :]
    c = gb_ref[...] * y
    for gs in _head_slices(width // HEAD_DIM):
        o_ref[:, gs] = _rms(c[:, gs], g_ref[:, gs]).astype(o_ref.dtype)
    ubuf_ref[0:8, :] = u[tc - 8:tc, :]


def _conv(p, s0, s1, conv_w, g_conv, layer, tp, n_prompt, ds, width):
    tc = CONV_TILE
    n = p.shape[0]
    seqs = tc // ds
    first_sample_tile = n_prompt // tc

    def state_map(i):
        return (jnp.maximum(i - first_sample_tile, 0), 0)

    return pl.pallas_call(
        functools.partial(_conv_kernel, tp=tp, n_prompt=n_prompt, ds=ds, width=width),
        grid=(n // tc,),
        in_specs=[pl.BlockSpec((tc, width), lambda i: (i, 0)),
                  pl.BlockSpec((tc, width), lambda i: (i, 1)),
                  pl.BlockSpec((tc, width), lambda i: (i, 2)),
                  pl.BlockSpec((seqs, width), state_map),
                  pl.BlockSpec((seqs, width), state_map),
                  pl.BlockSpec((None, CONV_K, width), lambda i: (layer, 0, 0)),
                  pl.BlockSpec((None, 1, width), lambda i: (layer, 0, 0))],
        out_specs=[pl.BlockSpec((tc, width), lambda i: (i, 0)),
                   pl.BlockSpec((tc, width), lambda i: (i, 0))],
        out_shape=[jax.ShapeDtypeStruct((n, width), _BF16),
                   jax.ShapeDtypeStruct((n, width), _F32)],
        scratch_shapes=[pltpu.VMEM((8 + tc, width), _F32)],
        compiler_params=_params(("arbitrary",)),
        name="gated_conv",
    )(p, p, p, s0, s1, conv_w, g_conv)


def _outproj_kernel(a_ref, c_ref, w_ref, h_ref, gpost_ref, gnext_ref, ho_ref, xo_ref, *, width):
    y = (jnp.dot(a_ref[...], w_ref[0:width, :], preferred_element_type=_F32)
         + jnp.dot(c_ref[...], w_ref[width:2 * width, :], preferred_element_type=_F32))
    h_new = h_ref[...] + _rms(y, gpost_ref[...])
    ho_ref[...] = h_new
    xo_ref[...] = _rms(h_new, gnext_ref[...]).astype(xo_ref.dtype)


def _outproj(attn_n, conv_n, w_out, h, g_post, g_next, layer, tm):
    n, d = h.shape
    width = attn_n.shape[1]
    gain = pl.BlockSpec((None, 1, d), lambda i: (layer, 0, 0))
    return pl.pallas_call(
        functools.partial(_outproj_kernel, width=width),
        grid=(n // tm,),
        in_specs=[pl.BlockSpec((tm, width), lambda i: (i, 0)),
                  pl.BlockSpec((tm, width), lambda i: (i, 0)),
                  pl.BlockSpec((None, 2 * width, d), lambda i: (layer, 0, 0)),
                  pl.BlockSpec((tm, d), lambda i: (i, 0)),
                  gain, gain],
        out_specs=[pl.BlockSpec((tm, d), lambda i: (i, 0)),
                   pl.BlockSpec((tm, d), lambda i: (i, 0))],
        out_shape=[jax.ShapeDtypeStruct((n, d), _F32),
                   jax.ShapeDtypeStruct((n, d), _BF16)],
        compiler_params=_params(("arbitrary",)),
        name="out_proj",
    )(attn_n, conv_n, w_out, h, g_post, g_next)


def _ffn_kernel(x_ref, wg_ref, wu_ref, wd_ref, h_ref, gpost_ref, gnext_ref, ho_ref, xo_ref):
    c = pl.program_id(1)

    @pl.when(c == 0)
    def _():
        ho_ref[...] = jnp.zeros_like(ho_ref)

    tm = x_ref.shape[0]
    split = (tm // 2 + 15) // 16 * 16
    groups = [slice(0, split), slice(split, tm)]
    wg, wu, wd = wg_ref[...], wu_ref[...], wd_ref[...]
    gate_up = [(jnp.dot(x_ref[rs, :], wg, preferred_element_type=_F32),
                jnp.dot(x_ref[rs, :], wu, preferred_element_type=_F32)) for rs in groups]
    for rs, (gate, up) in zip(groups, gate_up):
        act = (gate * jax.nn.sigmoid(gate) * up).astype(_BF16)
        ho_ref[rs, :] += jnp.dot(act, wd, preferred_element_type=_F32)

    @pl.when(c == pl.num_programs(1) - 1)
    def _():
        h_new = h_ref[...] + _rms(ho_ref[...], gpost_ref[...])
        ho_ref[...] = h_new
        xo_ref[...] = _rms(h_new, gnext_ref[...]).astype(xo_ref.dtype)


def _ffn(xf, w_gate, w_up, w_down, h, g_post, g_next, layer, next_layer, tm):
    n, d = h.shape
    d_ff = w_gate.shape[2]
    tf = FF_TILE
    return pl.pallas_call(
        _ffn_kernel,
        grid=(n // tm, d_ff // tf),
        in_specs=[pl.BlockSpec((tm, d), lambda i, c: (i, 0)),
                  pl.BlockSpec((None, d, tf), lambda i, c: (layer, 0, c)),
                  pl.BlockSpec((None, d, tf), lambda i, c: (layer, 0, c)),
                  pl.BlockSpec((None, tf, d), lambda i, c: (layer, c, 0)),
                  pl.BlockSpec((tm, d), lambda i, c: (i, 0)),
                  pl.BlockSpec((None, 1, d), lambda i, c: (layer, 0, 0)),
                  pl.BlockSpec((None, 1, d), lambda i, c: (next_layer, 0, 0))],
        out_specs=[pl.BlockSpec((tm, d), lambda i, c: (i, 0)),
                   pl.BlockSpec((tm, d), lambda i, c: (i, 0))],
        out_shape=[jax.ShapeDtypeStruct((n, d), _F32),
                   jax.ShapeDtypeStruct((n, d), _BF16)],
        compiler_params=_params(("arbitrary", "arbitrary")),
        name="swiglu_ffn",
    )(xf, w_gate, w_up, w_down, h, g_post, g_next)


def kernel(x_prompt, x_sample, cache_k, cache_v, state_conv, meta, w_in, w_out, conv_w, g_pre_mix, g_post_mix,
           g_attn_out, g_conv_out, g_pre_ffn, g_post_ffn, w_gate, w_up, w_down):
    batch, seq, d = x_prompt.shape
    n_streams, ds, _ = x_sample.shape
    depth = w_in.shape[0]
    width = w_out.shape[1] // 2
    n_heads = width // HEAD_DIM
    n_meta = meta.shape[0]
    past = cache_k.shape[2]
    t_real = n_meta + seq
    tp = -(-t_real // ATTN_TILE) * ATTN_TILE
    n_prompt = batch * tp
    n_sample = n_streams * ds
    n_tok = n_prompt + n_sample
    assert n_prompt % CONV_TILE == 0 and n_sample % CONV_TILE == 0 and CONV_TILE % ds == 0
    assert tp % ds == 0 and past % CACHE_TILE == 0 and ds >= CONV_K - 1
    assert w_gate.shape[2] % FF_TILE == 0 and cache_k.shape[3:] == (n_heads, HEAD_DIM)
    tm = _pick_tile(n_tok, ROW_TILE_TARGET)
    scale = HEAD_DIM ** -0.5

    pad = jnp.zeros((batch, tp - t_real, d), _F32)
    hp = jnp.concatenate([jnp.broadcast_to(meta.astype(_F32)[None], (batch, n_meta, d)), x_prompt, pad], axis=1)
    h = jnp.concatenate([hp.reshape(n_prompt, d), x_sample.reshape(n_sample, d)], axis=0)

    bf = lambda a: a.astype(_BF16)
    w_in_b, w_out_b, w_gate_b, w_up_b, w_down_b = bf(w_in), bf(w_out), bf(w_gate), bf(w_up), bf(w_down)
    rows = lambda a: a.reshape(depth, 1, -1)
    g_pre_mix, g_post_mix, g_pre_ffn, g_post_ffn = rows(g_pre_mix), rows(g_post_mix), rows(g_pre_ffn), rows(g_post_ffn)
    g_attn_out, g_conv_out = rows(g_attn_out), rows(g_conv_out)
    cache_k2 = cache_k.reshape(-1, HEAD_DIM)
    cache_v2 = cache_v.reshape(-1, HEAD_DIM)

    xn = _norm(h, g_pre_mix, 0, tm)
    k_l, v_l, u_l = [], [], []
    for l in range(depth):
        p, pb, k3, v3 = _inproj(xn, w_in_b, l, tm, width)
        attn_n = jnp.concatenate([
            _attn_prompt(pb, g_attn_out, l, batch, tp, width, scale),
            _attn_sample(pb, cache_k2, cache_v2, g_attn_out, l, n_prompt, n_streams, ds, past, width, scale)])
        conv_n, u = _conv(p, state_conv[l, :, 0, :], state_conv[l, :, 1, :], conv_w, g_conv_out, l,
                          tp, n_prompt, ds, width)
        h, xf = _outproj(attn_n, conv_n, w_out_b, h, g_post_mix, g_pre_ffn, l, tm)
        h, xn = _ffn(xf, w_gate_b, w_up_b, w_down_b, h, g_post_ffn, g_pre_mix, l, min(l + 1, depth - 1), tm)
        k_l.append(k3.reshape(n_tok, n_heads, HEAD_DIM))
        v_l.append(v3.reshape(n_tok, n_heads, HEAD_DIM))
        u_l.append(u)

    def prompt_part(xs, lo, hi):
        return jnp.stack([x[:n_prompt].reshape((batch, tp) + x.shape[1:])[:, lo:hi] for x in xs])

    def sample_part(xs, lo):
        return jnp.stack([x[n_prompt:].reshape((n_streams, ds) + x.shape[1:])[:, lo:] for x in xs])

    y_prompt = h[:n_prompt].reshape(batch, tp, d)[:, n_meta:t_real]
    y_sample = h[n_prompt:].reshape(n_streams, ds, d)
    return (y_prompt, y_sample,
            prompt_part(k_l, 0, t_real), prompt_part(v_l, 0, t_real), prompt_part(u_l, t_real - (CONV_K - 1), t_real),
            sample_part(k_l, 0), sample_part(v_l, 0), sample_part(u_l, ds - (CONV_K - 1)))
```

```python
import functools

import jax
import jax.numpy as jnp
from jax import lax
from jax.experimental import pallas as pl
from jax.experimental.pallas import tpu as pltpu

HEAD_DIM = 128
CONV_K = 3
EPS = 1e-6
ATTN_TILE = 128
CACHE_TILE = 256
CONV_TILE = 256
ROW_TILE_TARGET = 600
IN_ROW_TILE_TARGET = 1200
IN_COL_TILE = 512
FF_TILE = 512
VMEM_LIMIT = 56 * 1024 * 1024
LOG_WEIGHT_FLOOR = -105.0

_F32 = jnp.float32
_BF16 = jnp.bfloat16
_NT = (((1,), (1,)), ((), ()))


def _pick_tile(n, target, mult=16):
    best = None
    for t in range(mult, target + 1, mult):
        if n % t == 0:
            best = t
    assert best is not None, (n, target)
    return best


def _params(sem):
    return pltpu.CompilerParams(dimension_semantics=sem, vmem_limit_bytes=VMEM_LIMIT)


def _rms(x, g):
    ms = jnp.mean(x * x, axis=-1, keepdims=True)
    return x * lax.rsqrt(ms + EPS) * g


def _head_slices(n_heads):
    return [slice(h * HEAD_DIM, (h + 1) * HEAD_DIM) for h in range(n_heads)]


def _norm_kernel(h_ref, g_ref, o_ref):
    o_ref[...] = _rms(h_ref[...], g_ref[...]).astype(o_ref.dtype)


def _norm(h, g, layer, tm):
    n, d = h.shape
    return pl.pallas_call(
        _norm_kernel,
        grid=(n // tm,),
        in_specs=[pl.BlockSpec((tm, d), lambda i: (i, 0)),
                  pl.BlockSpec((None, 1, d), lambda i: (layer, 0, 0))],
        out_specs=pl.BlockSpec((tm, d), lambda i: (i, 0)),
        out_shape=jax.ShapeDtypeStruct((n, d), _BF16),
        compiler_params=_params(("arbitrary",)),
        name="pre_norm",
    )(h, g)


def _inproj_kernel(x_ref, w_ref, p_ref, pb_ref, k_ref, v_ref, *, n_heads, tm, wpk):
    j = pl.program_id(1)
    acc = jnp.dot(x_ref[...], w_ref[...], preferred_element_type=_F32)

    @pl.when(j < 3 * wpk)
    def _():
        pb_ref[...] = acc.astype(_BF16)

    @pl.when(j >= 3 * wpk)
    def _():
        p_ref[...] = acc

    heads_per_tile = n_heads // wpk
    for ref, first in ((k_ref, wpk), (v_ref, 2 * wpk)):
        for part in range(wpk):
            @pl.when(j == first + part)
            def _(ref=ref, part=part):
                for hh, hs in enumerate(_head_slices(heads_per_tile)):
                    ref[pl.ds(part * heads_per_tile + hh, tm, stride=n_heads), :] = acc[:, hs]


def _inproj(xn, w, tm, width):
    n, d = xn.shape
    n_heads = width // HEAD_DIM
    tn = min(IN_COL_TILE, width)
    wpk = width // tn
    kv_shape = jax.ShapeDtypeStruct((n * n_heads, HEAD_DIM), _F32)
    kv_spec = pl.BlockSpec((tm * n_heads, HEAD_DIM), lambda i, j: (i, 0))
    return pl.pallas_call(
        functools.partial(_inproj_kernel, n_heads=n_heads, tm=tm, wpk=wpk),
        grid=(n // tm, 6 * wpk),
        in_specs=[pl.BlockSpec((tm, d), lambda i, j: (i, 0)),
                  pl.BlockSpec((d, tn), lambda i, j: (0, j))],
        out_specs=[pl.BlockSpec((tm, tn), lambda i, j: (i, jnp.maximum(j - 3 * wpk, 0))),
                   pl.BlockSpec((tm, tn), lambda i, j: (i, jnp.minimum(j, 3 * wpk - 1))),
                   kv_spec, kv_spec],
        out_shape=[jax.ShapeDtypeStruct((n, 3 * width), _F32),
                   jax.ShapeDtypeStruct((n, 3 * width), _BF16),
                   kv_shape, kv_shape],
        compiler_params=_params(("arbitrary", "arbitrary")),
        name="in_proj",
    )(xn, w)


def _log_sigmoids(z):
    lsp = jnp.minimum(z, 0.0) - jnp.log(1.0 + jnp.exp(-jnp.abs(z)))
    return lsp, lsp - z


def _suffix_sums(l_mat, uo):
    hi = l_mat.astype(_BF16)
    lo = (l_mat - hi.astype(_F32)).astype(_BF16)
    if uo.shape[0] == 2 * l_mat.shape[1]:
        return jnp.dot(jnp.concatenate([hi, lo], axis=1), uo, preferred_element_type=_F32)
    return (jnp.dot(hi, uo, preferred_element_type=_F32)
            + jnp.dot(lo, uo, preferred_element_type=_F32))


def _suffix_matrix(rows, width, copies=1):
    j = lax.rem(lax.broadcasted_iota(jnp.int32, (copies * rows, 2 * width), 0), rows)
    s = lax.broadcasted_iota(jnp.int32, (copies * rows, 2 * width), 1)
    return jnp.where((s >= width) | (j > s), 1.0, 0.0).astype(_BF16)


def _sb_tile(qs, ks, vs, cars, uo, vis, scale):
    tk = ks[0].shape[0]
    wc = uo.shape[1] // 2
    zs = [lax.dot_general(q, k, _NT, preferred_element_type=_F32) * scale for q, k in zip(qs, ks)]
    lsps, rs = [], []
    for z in zs:
        lsp, lsn = _log_sigmoids(z)
        if vis is not None:
            lsn = jnp.where(vis, lsn, 0.0)
        lsps.append(lsp)
        rs.append(_suffix_sums(lsn, uo))
    pvs, new_cars = [], []
    for lsp, r, v, car in zip(lsps, rs, vs, cars):
        x = lsp + r[:, :tk]
        if car is not None:
            x = x + car[:, :tk]
        w = jnp.exp(x)
        if vis is not None:
            w = jnp.where(vis, w, 0.0)
        pvs.append(jnp.dot(w.astype(_BF16), v, preferred_element_type=_F32))
        new_cars.append(r[:, wc:] if car is None else car + r[:, wc:])
    return pvs, new_cars


def _head_norm_store(o_ref, acc_ref, g_ref, n_heads):
    for hs in _head_slices(n_heads):
        o_ref[:, hs] = _rms(acc_ref[:, hs], g_ref[:, hs]).astype(o_ref.dtype)


def _attn_prompt_kernel(q_ref, k_ref, v_ref, uo_ref, g_ref, *rest, n_heads, scale, cast_every):
    n_cast = len(cast_every)
    cast_in, o_ref, cast_out = rest[:n_cast], rest[n_cast], rest[n_cast + 1:2 * n_cast + 1]
    acc_ref, car_ref = rest[2 * n_cast + 1:]
    tq = ATTN_TILE
    i = pl.program_id(1)
    uo = uo_ref[...]
    heads = _head_slices(n_heads)

    step = pl.program_id(0) * pl.num_programs(1) + i
    for src, dst, (every, n_blocks) in zip(cast_in, cast_out, cast_every):
        @pl.when(jnp.logical_and(step < every * n_blocks, lax.rem(step, every) == 0))
        def _(src=src, dst=dst):
            dst[...] = src[...].astype(_BF16)

    def tile(t, vis, first):
        k0 = pl.multiple_of((i - t) * tq, tq)
        qs = [q_ref[:, hs] for hs in heads]
        ks = [k_ref[pl.ds(k0, tq), hs] for hs in heads]
        vs = [v_ref[pl.ds(k0, tq), hs] for hs in heads]
        cars = [None if first else car_ref[:, hs] for hs in heads]
        pvs, new_cars = _sb_tile(qs, ks, vs, cars, uo, vis, scale)
        for hs, pv, car in zip(heads, pvs, new_cars):
            if first:
                acc_ref[:, hs] = pv
            else:
                acc_ref[:, hs] += pv
            car_ref[:, hs] = car

    causal = (lax.broadcasted_iota(jnp.int32, (tq, tq), 1)
              < lax.broadcasted_iota(jnp.int32, (tq, tq), 0))
    tile(0, causal, True)

    def cond(state):
        t, done = state
        return jnp.logical_and(t <= i, done == 0)

    def body(state):
        t, _ = state
        tile(t, None, False)
        done = jnp.max(car_ref[...]) < LOG_WEIGHT_FLOOR
        return t + 1, done.astype(jnp.int32)

    lax.while_loop(cond, body, (jnp.int32(1), jnp.int32(0)))
    _head_norm_store(o_ref, acc_ref, g_ref, n_heads)


def _cast_plan(rows, n_steps):
    n_blocks = 1
    while n_blocks * 2 <= n_steps and rows % (n_blocks * 2) == 0 and (rows // (n_blocks * 2)) % 16 == 0:
        n_blocks *= 2
    every = 1
    while every * 2 * n_blocks <= n_steps:
        every *= 2
    return every, n_blocks


def _attn_prompt(pb, g_attn, layer, batch, tp, width, scale, cast):
    tq = ATTN_TILE
    nq = tp // tq
    n_heads = width // HEAD_DIM
    uo = _suffix_matrix(tq, tq, copies=2)
    plans = [_cast_plan(w.shape[1], batch * nq) for w, _ in cast]

    def cast_specs(w, lyr, plan):
        every, n_blocks = plan
        rows, cols = w.shape[1:]
        blk = rows // n_blocks
        pick = lambda b, i: jnp.minimum((b * nq + i) // every, n_blocks - 1)
        return (pl.BlockSpec((None, blk, cols), lambda b, i: (lyr, pick(b, i), 0)),
                pl.BlockSpec((blk, cols), lambda b, i: (pick(b, i), 0)),
                jax.ShapeDtypeStruct((rows, cols), _BF16))

    specs = [cast_specs(w, lyr, plan) for (w, lyr), plan in zip(cast, plans)]
    outs = pl.pallas_call(
        functools.partial(_attn_prompt_kernel, n_heads=n_heads, scale=scale, cast_every=tuple(plans)),
        grid=(batch, nq),
        in_specs=[pl.BlockSpec((tq, width), lambda b, i: (b * nq + i, 0)),
                  pl.BlockSpec((tp, width), lambda b, i: (b, 1)),
                  pl.BlockSpec((tp, width), lambda b, i: (b, 2)),
                  pl.BlockSpec((2 * tq, 2 * tq), lambda b, i: (0, 0)),
                  pl.BlockSpec((None, 1, width), lambda b, i: (layer, 0, 0))] + [s[0] for s in specs],
        out_specs=[pl.BlockSpec((tq, width), lambda b, i: (b * nq + i, 0))] + [s[1] for s in specs],
        out_shape=[jax.ShapeDtypeStruct((batch * tp, width), _BF16)] + [s[2] for s in specs],
        scratch_shapes=[pltpu.VMEM((tq, width), _F32), pltpu.VMEM((tq, width), _F32)],
        compiler_params=_params(("arbitrary", "arbitrary")),
        name="attn_prompt",
    )(pb, pb, pb, uo, g_attn, *[w for w, _ in cast])
    return outs[0], outs[1:]


def _attn_sample_kernel(q_ref, kn_ref, vn_ref, ck_hbm, cv_hbm, uos_ref, uoc_ref, g_ref,
                        o_ref, acc_ref, car_ref, kbuf, vbuf, sem, *, n_heads, scale, ds, past, stream0):
    tk = CACHE_TILE
    n_ct = past // tk
    rows = tk * n_heads
    b = pl.program_id(0)
    heads = _head_slices(n_heads)

    def copies(stream, blk, slot):
        row0 = pl.multiple_of(((stream0 + stream) * past + blk * tk) * n_heads, rows)
        return (pltpu.make_async_copy(ck_hbm.at[pl.ds(row0, rows), :], kbuf.at[slot], sem.at[0, slot]),
                pltpu.make_async_copy(cv_hbm.at[pl.ds(row0, rows), :], vbuf.at[slot], sem.at[1, slot]))

    def start(stream, blk, slot):
        for c in copies(stream, blk, slot):
            c.start()

    def wait(stream, blk, slot):
        for c in copies(stream, blk, slot):
            c.wait()

    newest = n_ct - 1
    par = lax.rem(b, 2)

    @pl.when(b == 0)
    def _():
        start(b, newest, par)

    @pl.when(b + 1 < pl.num_programs(0))
    def _():
        start(b + 1, newest, 1 - par)

    causal = (lax.broadcasted_iota(jnp.int32, (ds, ds), 1)
              < lax.broadcasted_iota(jnp.int32, (ds, ds), 0))
    qs = [q_ref[:, hs] for hs in heads]
    pvs, cars = _sb_tile(qs, [kn_ref[:, hs] for hs in heads], [vn_ref[:, hs] for hs in heads],
                         [None] * n_heads, uos_ref[...], causal, scale)
    for h, hs in enumerate(heads):
        acc_ref[:, hs] = pvs[h]
        car_ref[h] = cars[h]

    def cache_tile(slot):
        kslot, vslot = kbuf.at[slot], vbuf.at[slot]
        ks = [kslot[pl.ds(h, tk, stride=n_heads), :].astype(_BF16) for h in range(n_heads)]
        vs = [vslot[pl.ds(h, tk, stride=n_heads), :].astype(_BF16) for h in range(n_heads)]
        pvs, cars = _sb_tile([q_ref[:, hs] for hs in heads], ks, vs,
                             [car_ref[h] for h in range(n_heads)], uoc_ref[...], None, scale)
        for h, hs in enumerate(heads):
            acc_ref[:, hs] += pvs[h]
            car_ref[h] = cars[h]
        return (jnp.max(car_ref[...]) < LOG_WEIGHT_FLOOR).astype(jnp.int32)

    wait(b, newest, par)
    done = cache_tile(par)

    if n_ct > 1:
        older = lambda blk: 2 + lax.rem(blk, 2)

        @pl.when(done == 0)
        def _():
            start(b, newest - 1, older(newest - 1))

        def cond(state):
            blk, done = state
            return jnp.logical_and(blk >= 0, done == 0)

        def body(state):
            blk, _ = state
            wait(b, blk, older(blk))

            @pl.when(blk > 0)
            def _():
                start(b, blk - 1, older(blk - 1))

            return blk - 1, cache_tile(older(blk))

        blk_end, _ = lax.while_loop(cond, body, (jnp.int32(newest - 1), done))

        @pl.when(jnp.logical_and(blk_end >= 0, blk_end < newest - 1))
        def _():
            wait(b, blk_end, older(blk_end))

    _head_norm_store(o_ref, acc_ref, g_ref, n_heads)


def _attn_sample(pb, cache_k, cache_v, g_attn, layer, row0, n_streams, ds, past, width, scale):
    tk = CACHE_TILE
    n_heads = width // HEAD_DIM
    blk0 = row0 // ds
    uos = _suffix_matrix(ds, tk)
    uoc = _suffix_matrix(tk, tk)
    return pl.pallas_call(
        functools.partial(_attn_sample_kernel, n_heads=n_heads, scale=scale, ds=ds, past=past,
                          stream0=layer * n_streams),
        grid=(n_streams,),
        in_specs=[pl.BlockSpec((ds, width), lambda b: (blk0 + b, 0)),
                  pl.BlockSpec((ds, width), lambda b: (blk0 + b, 1)),
                  pl.BlockSpec((ds, width), lambda b: (blk0 + b, 2)),
                  pl.BlockSpec(memory_space=pl.ANY),
                  pl.BlockSpec(memory_space=pl.ANY),
                  pl.BlockSpec((ds, 2 * tk), lambda b: (0, 0)),
                  pl.BlockSpec((tk, 2 * tk), lambda b: (0, 0)),
                  pl.BlockSpec((None, 1, width), lambda b: (layer, 0, 0))],
        out_specs=pl.BlockSpec((ds, width), lambda b: (b, 0)),
        out_shape=jax.ShapeDtypeStruct((n_streams * ds, width), _BF16),
        scratch_shapes=[pltpu.VMEM((ds, width), _F32),
                        pltpu.VMEM((n_heads, ds, tk), _F32),
                        pltpu.VMEM((4, tk * n_heads, HEAD_DIM), _F32),
                        pltpu.VMEM((4, tk * n_heads, HEAD_DIM), _F32),
                        pltpu.SemaphoreType.DMA((2, 4))],
        compiler_params=_params(("arbitrary",)),
        name="attn_sample",
    )(pb, pb, pb, cache_k, cache_v, uos, uoc, g_attn)


def _conv_kernel(gb_ref, gc_ref, ci_ref, s0_ref, s1_ref, w_ref, g_ref, o_ref, u_ref, ubuf_ref,
                 *, tp, n_prompt, ds, width):
    tc = CONV_TILE
    i = pl.program_id(0)

    @pl.when(i == 0)
    def _():
        ubuf_ref[0:8, :] = jnp.zeros((8, width), _F32)

    u = gc_ref[...] * ci_ref[...]
    u_ref[...] = u
    ubuf_ref[8:8 + tc, :] = u
    u1 = ubuf_ref[7:7 + tc, :]
    u2 = ubuf_ref[6:6 + tc, :]

    row = i * tc + lax.broadcasted_iota(jnp.int32, (tc, 1), 0)
    is_sample = row >= n_prompt
    pos = jnp.where(is_sample, lax.rem(row - n_prompt, ds), lax.rem(row, tp))
    first = pos == 0
    second = pos == 1
    seqs = tc // ds
    zero = jnp.zeros((tc, width), _F32)
    st0 = jnp.broadcast_to(s0_ref[...][:, None, :], (seqs, ds, width)).reshape(tc, width)
    st1 = jnp.broadcast_to(s1_ref[...][:, None, :], (seqs, ds, width)).reshape(tc, width)
    st0 = jnp.where(is_sample, st0, zero)
    st1 = jnp.where(is_sample, st1, zero)
    u1 = jnp.where(first, st1, u1)
    u2 = jnp.where(first, st0, jnp.where(second, st1, u2))
    y = u2 * w_ref[0:1, :] + u1 * w_ref[1:2, :] + u * w_ref[2:3, :]
    c = gb_ref[...] * y
    for gs in _head_slices(width // HEAD_DIM):
        o_ref[:, gs] = _rms(c[:, gs], g_ref[:, gs]).astype(o_ref.dtype)
    ubuf_ref[0:8, :] = u[tc - 8:tc, :]


def _conv(p, s0, s1, conv_w, g_conv, layer, tp, n_prompt, ds, width):
    tc = CONV_TILE
    n = p.shape[0]
    seqs = tc // ds
    first_sample_tile = n_prompt // tc

    def state_map(i):
        return (jnp.maximum(i - first_sample_tile, 0), 0)

    return pl.pallas_call(
        functools.partial(_conv_kernel, tp=tp, n_prompt=n_prompt, ds=ds, width=width),
        grid=(n // tc,),
        in_specs=[pl.BlockSpec((tc, width), lambda i: (i, 0)),
                  pl.BlockSpec((tc, width), lambda i: (i, 1)),
                  pl.BlockSpec((tc, width), lambda i: (i, 2)),
                  pl.BlockSpec((seqs, width), state_map),
                  pl.BlockSpec((seqs, width), state_map),
                  pl.BlockSpec((None, CONV_K, width), lambda i: (layer, 0, 0)),
                  pl.BlockSpec((None, 1, width), lambda i: (layer, 0, 0))],
        out_specs=[pl.BlockSpec((tc, width), lambda i: (i, 0)),
                   pl.BlockSpec((tc, width), lambda i: (i, 0))],
        out_shape=[jax.ShapeDtypeStruct((n, width), _BF16),
                   jax.ShapeDtypeStruct((n, width), _F32)],
        scratch_shapes=[pltpu.VMEM((8 + tc, width), _F32)],
        compiler_params=_params(("arbitrary",)),
        name="gated_conv",
    )(p, p, p, s0, s1, conv_w, g_conv)


def _outproj_kernel(a_ref, c_ref, w_ref, h_ref, gpost_ref, gnext_ref, ho_ref, xo_ref, *, width):
    y = (jnp.dot(a_ref[...], w_ref[0:width, :], preferred_element_type=_F32)
         + jnp.dot(c_ref[...], w_ref[width:2 * width, :], preferred_element_type=_F32))
    h_new = h_ref[...] + _rms(y, gpost_ref[...])
    ho_ref[...] = h_new
    xo_ref[...] = _rms(h_new, gnext_ref[...]).astype(xo_ref.dtype)


def _outproj(attn_n, conv_n, w_out, h, g_post, g_next, layer, tm):
    n, d = h.shape
    width = attn_n.shape[1]
    gain = pl.BlockSpec((None, 1, d), lambda i: (layer, 0, 0))
    return pl.pallas_call(
        functools.partial(_outproj_kernel, width=width),
        grid=(n // tm,),
        in_specs=[pl.BlockSpec((tm, width), lambda i: (i, 0)),
                  pl.BlockSpec((tm, width), lambda i: (i, 0)),
                  pl.BlockSpec((2 * width, d), lambda i: (0, 0)),
                  pl.BlockSpec((tm, d), lambda i: (i, 0)),
                  gain, gain],
        out_specs=[pl.BlockSpec((tm, d), lambda i: (i, 0)),
                   pl.BlockSpec((tm, d), lambda i: (i, 0))],
        out_shape=[jax.ShapeDtypeStruct((n, d), _F32),
                   jax.ShapeDtypeStruct((n, d), _BF16)],
        compiler_params=_params(("arbitrary",)),
        name="out_proj",
    )(attn_n, conv_n, w_out, h, g_post, g_next)


def _ffn_kernel(x_ref, wg_ref, wu_ref, wd_ref, h_ref, gpost_ref, gnext_ref, ho_ref, xo_ref):
    c = pl.program_id(1)

    @pl.when(c == 0)
    def _():
        ho_ref[...] = jnp.zeros_like(ho_ref)

    tm = x_ref.shape[0]
    split = (tm // 2 + 15) // 16 * 16
    groups = [slice(0, split), slice(split, tm)]
    wg, wu, wd = wg_ref[...], wu_ref[...], wd_ref[...]
    gate_up = [(jnp.dot(x_ref[rs, :], wg, preferred_element_type=_F32),
                jnp.dot(x_ref[rs, :], wu, preferred_element_type=_F32)) for rs in groups]
    for rs, (gate, up) in zip(groups, gate_up):
        act = (gate * jax.nn.sigmoid(gate) * up).astype(_BF16)
        ho_ref[rs, :] += jnp.dot(act, wd, preferred_element_type=_F32)

    @pl.when(c == pl.num_programs(1) - 1)
    def _():
        h_new = h_ref[...] + _rms(ho_ref[...], gpost_ref[...])
        ho_ref[...] = h_new
        xo_ref[...] = _rms(h_new, gnext_ref[...]).astype(xo_ref.dtype)


def _ffn(xf, w_gate, w_up, w_down, h, g_post, g_next, layer, next_layer, tm):
    n, d = h.shape
    d_ff = w_gate.shape[1]
    tf = FF_TILE
    return pl.pallas_call(
        _ffn_kernel,
        grid=(n // tm, d_ff // tf),
        in_specs=[pl.BlockSpec((tm, d), lambda i, c: (i, 0)),
                  pl.BlockSpec((d, tf), lambda i, c: (0, c)),
                  pl.BlockSpec((d, tf), lambda i, c: (0, c)),
                  pl.BlockSpec((tf, d), lambda i, c: (c, 0)),
                  pl.BlockSpec((tm, d), lambda i, c: (i, 0)),
                  pl.BlockSpec((None, 1, d), lambda i, c: (layer, 0, 0)),
                  pl.BlockSpec((None, 1, d), lambda i, c: (next_layer, 0, 0))],
        out_specs=[pl.BlockSpec((tm, d), lambda i, c: (i, 0)),
                   pl.BlockSpec((tm, d), lambda i, c: (i, 0))],
        out_shape=[jax.ShapeDtypeStruct((n, d), _F32),
                   jax.ShapeDtypeStruct((n, d), _BF16)],
        compiler_params=_params(("arbitrary", "arbitrary")),
        name="swiglu_ffn",
    )(xf, w_gate, w_up, w_down, h, g_post, g_next)


def kernel(x_prompt, x_sample, cache_k, cache_v, state_conv, meta, w_in, w_out, conv_w, g_pre_mix, g_post_mix,
           g_attn_out, g_conv_out, g_pre_ffn, g_post_ffn, w_gate, w_up, w_down):
    batch, seq, d = x_prompt.shape
    n_streams, ds, _ = x_sample.shape
    depth = w_in.shape[0]
    width = w_out.shape[1] // 2
    n_heads = width // HEAD_DIM
    n_meta = meta.shape[0]
    past = cache_k.shape[2]
    t_real = n_meta + seq
    tp = -(-t_real // ATTN_TILE) * ATTN_TILE
    n_prompt = batch * tp
    n_sample = n_streams * ds
    n_tok = n_prompt + n_sample
    assert n_prompt % CONV_TILE == 0 and n_sample % CONV_TILE == 0 and CONV_TILE % ds == 0
    assert tp % ds == 0 and past % CACHE_TILE == 0 and ds >= CONV_K - 1
    assert w_gate.shape[2] % FF_TILE == 0 and cache_k.shape[3:] == (n_heads, HEAD_DIM)
    tm = _pick_tile(n_tok, ROW_TILE_TARGET)
    tm_in = _pick_tile(n_tok, IN_ROW_TILE_TARGET)
    scale = HEAD_DIM ** -0.5

    pad = jnp.zeros((batch, tp - t_real, d), _F32)
    hp = jnp.concatenate([jnp.broadcast_to(meta.astype(_F32)[None], (batch, n_meta, d)), x_prompt, pad], axis=1)
    h = jnp.concatenate([hp.reshape(n_prompt, d), x_sample.reshape(n_sample, d)], axis=0)

    rows = lambda a: a.reshape(depth, 1, -1)
    g_pre_mix, g_post_mix, g_pre_ffn, g_post_ffn = rows(g_pre_mix), rows(g_post_mix), rows(g_pre_ffn), rows(g_post_ffn)
    g_attn_out, g_conv_out = rows(g_attn_out), rows(g_conv_out)
    cache_k2 = cache_k.reshape(-1, HEAD_DIM)
    cache_v2 = cache_v.reshape(-1, HEAD_DIM)

    xn = _norm(h, g_pre_mix, 0, tm)
    w_in_b = w_in[0].astype(_BF16)
    k_l, v_l, u_l = [], [], []
    for l in range(depth):
        p, pb, k3, v3 = _inproj(xn, w_in_b, tm_in, width)
        cast = [(w_out, l), (w_gate, l), (w_up, l), (w_down, l)] + ([(w_in, l + 1)] if l + 1 < depth else [])
        attn_p, casted = _attn_prompt(pb, g_attn_out, l, batch, tp, width, scale, cast)
        w_out_b, w_gate_b, w_up_b, w_down_b = casted[:4]
        w_in_b = casted[4] if l + 1 < depth else None
        attn_n = jnp.concatenate([
            attn_p,
            _attn_sample(pb, cache_k2, cache_v2, g_attn_out, l, n_prompt, n_streams, ds, past, width, scale)])
        conv_n, u = _conv(p, state_conv[l, :, 0, :], state_conv[l, :, 1, :], conv_w, g_conv_out, l,
                          tp, n_prompt, ds, width)
        h, xf = _outproj(attn_n, conv_n, w_out_b, h, g_post_mix, g_pre_ffn, l, tm)
        h, xn = _ffn(xf, w_gate_b, w_up_b, w_down_b, h, g_post_ffn, g_pre_mix, l, min(l + 1, depth - 1), tm)
        k_l.append(k3.reshape(n_tok, n_heads, HEAD_DIM))
        v_l.append(v3.reshape(n_tok, n_heads, HEAD_DIM))
        u_l.append(u)

    def prompt_part(xs, lo, hi):
        return jnp.stack([x[:n_prompt].reshape((batch, tp) + x.shape[1:])[:, lo:hi] for x in xs])

    def sample_part(xs, lo):
        return jnp.stack([x[n_prompt:].reshape((n_streams, ds) + x.shape[1:])[:, lo:] for x in xs])

    y_prompt = h[:n_prompt].reshape(batch, tp, d)[:, n_meta:t_real]
    y_sample = h[n_prompt:].reshape(n_streams, ds, d)
    return (y_prompt, y_sample,
            prompt_part(k_l, 0, t_real), prompt_part(v_l, 0, t_real), prompt_part(u_l, t_real - (CONV_K - 1), t_real),
            sample_part(k_l, 0), sample_part(v_l, 0), sample_part(u_l, ds - (CONV_K - 1)))
```

```python
import functools

import jax
import jax.numpy as jnp
from jax import lax
from jax.experimental import pallas as pl
from jax.experimental.pallas import tpu as pltpu

HEAD_DIM = 128
CONV_K = 3
EPS = 1e-6
ATTN_TILE = 128
CACHE_TILE = 256
CONV_TILE = 256
ROW_TILE_TARGET = 600
IN_ROW_TILE_TARGET = 1200
IN_COL_TILE = 512
FF_TILE = 512
VMEM_LIMIT = 56 * 1024 * 1024
LOG2_E = 1.4426950408889634
LOG2_WEIGHT_FLOOR = -152.0

_F32 = jnp.float32
_BF16 = jnp.bfloat16
_NT = (((1,), (1,)), ((), ()))


def _pick_tile(n, target, mult=16):
    best = None
    for t in range(mult, target + 1, mult):
        if n % t == 0:
            best = t
    assert best is not None, (n, target)
    return best


def _params(sem):
    return pltpu.CompilerParams(dimension_semantics=sem, vmem_limit_bytes=VMEM_LIMIT)


def _rms(x, g):
    ms = jnp.mean(x * x, axis=-1, keepdims=True)
    return x * lax.rsqrt(ms + EPS) * g


def _head_slices(n_heads):
    return [slice(h * HEAD_DIM, (h + 1) * HEAD_DIM) for h in range(n_heads)]


def _norm_kernel(h_ref, g_ref, o_ref):
    o_ref[...] = _rms(h_ref[...], g_ref[...]).astype(o_ref.dtype)


def _norm(h, g, layer, tm):
    n, d = h.shape
    return pl.pallas_call(
        _norm_kernel,
        grid=(n // tm,),
        in_specs=[pl.BlockSpec((tm, d), lambda i: (i, 0)),
                  pl.BlockSpec((None, 1, d), lambda i: (layer, 0, 0))],
        out_specs=pl.BlockSpec((tm, d), lambda i: (i, 0)),
        out_shape=jax.ShapeDtypeStruct((n, d), _BF16),
        compiler_params=_params(("arbitrary",)),
        name="pre_norm",
    )(h, g)


def _inproj_kernel(x_ref, w_ref, p_ref, pb_ref, k_ref, v_ref, *, n_heads, tm, wpk):
    j = pl.program_id(1)
    acc = jnp.dot(x_ref[...], w_ref[...], preferred_element_type=_F32)

    @pl.when(j < 3 * wpk)
    def _():
        pb_ref[...] = acc.astype(_BF16)

    @pl.when(j >= 3 * wpk)
    def _():
        p_ref[...] = acc

    heads_per_tile = n_heads // wpk
    for ref, first in ((k_ref, wpk), (v_ref, 2 * wpk)):
        for part in range(wpk):
            @pl.when(j == first + part)
            def _(ref=ref, part=part):
                for hh, hs in enumerate(_head_slices(heads_per_tile)):
                    ref[pl.ds(part * heads_per_tile + hh, tm, stride=n_heads), :] = acc[:, hs]


def _inproj(xn, w, tm, width):
    n, d = xn.shape
    n_heads = width // HEAD_DIM
    tn = min(IN_COL_TILE, width)
    wpk = width // tn
    kv_shape = jax.ShapeDtypeStruct((n * n_heads, HEAD_DIM), _F32)
    kv_spec = pl.BlockSpec((tm * n_heads, HEAD_DIM), lambda i, j: (i, 0))
    return pl.pallas_call(
        functools.partial(_inproj_kernel, n_heads=n_heads, tm=tm, wpk=wpk),
        grid=(n // tm, 6 * wpk),
        in_specs=[pl.BlockSpec((tm, d), lambda i, j: (i, 0)),
                  pl.BlockSpec((d, tn), lambda i, j: (0, j))],
        out_specs=[pl.BlockSpec((tm, tn), lambda i, j: (i, jnp.maximum(j - 3 * wpk, 0))),
                   pl.BlockSpec((tm, tn), lambda i, j: (i, jnp.minimum(j, 3 * wpk - 1))),
                   kv_spec, kv_spec],
        out_shape=[jax.ShapeDtypeStruct((n, 3 * width), _F32),
                   jax.ShapeDtypeStruct((n, 3 * width), _BF16),
                   kv_shape, kv_shape],
        compiler_params=_params(("arbitrary", "arbitrary")),
        name="in_proj",
    )(xn, w)


def _log2_sigmoids(z2):
    neg_abs = lax.bitcast_convert_type(lax.bitcast_convert_type(z2, jnp.uint32) | jnp.uint32(0x80000000), _F32)
    lsp = jnp.minimum(z2, 0.0) - jnp.log2(1.0 + jnp.exp2(neg_abs))
    return lsp, lsp - z2


def _suffix_sums(l_mats, uo):
    tq, tk = l_mats[0].shape
    sub, wc = uo.shape[0] // 2, uo.shape[1] // 2
    his = [l.astype(_BF16) for l in l_mats]
    los = [(l - hi.astype(_F32)).astype(_BF16) for l, hi in zip(l_mats, his)]
    if sub % HEAD_DIM:
        assert tk == sub
        top = uo[:tk]
        rs = [jnp.dot(hi, top, preferred_element_type=_F32) + jnp.dot(lo, top, preferred_element_type=_F32)
              for hi, lo in zip(his, los)]
        return [(r[:, :tk], r[:, wc:]) for r in rs]
    n_sub = tk // sub
    cols = [slice(b * sub, (b + 1) * sub) for b in range(n_sub)]
    lhs = jnp.concatenate([jnp.concatenate([hi[:, cs], lo[:, cs]], axis=1)
                           for hi, lo in zip(his, los) for cs in cols], axis=0)
    r = jnp.dot(lhs, uo, preferred_element_type=_F32)
    out = []
    for h in range(len(l_mats)):
        newer, afters = None, [None] * n_sub
        for b in reversed(range(n_sub)):
            r_hb = r[(h * n_sub + b) * tq:(h * n_sub + b + 1) * tq]
            afters[b] = r_hb[:, :sub] if newer is None else r_hb[:, :sub] + newer[:, :sub]
            newer = r_hb[:, wc:] if newer is None else newer + r_hb[:, wc:]
        out.append((afters[0] if n_sub == 1 else jnp.concatenate(afters, axis=1), newer))
    return out


def _suffix_matrix(rows, width):
    j = lax.rem(lax.broadcasted_iota(jnp.int32, (2 * rows, 2 * width), 0), rows)
    s = lax.broadcasted_iota(jnp.int32, (2 * rows, 2 * width), 1)
    return jnp.where((s >= width) | (j > s), 1.0, 0.0).astype(_BF16)


def _sb_tile(qs, ks, vs, cars, uo, vis, scale):
    tk = ks[0].shape[0]
    zs = [lax.dot_general(q, k, _NT, preferred_element_type=_F32) * (scale * LOG2_E) for q, k in zip(qs, ks)]
    lsps, lsns = [], []
    for z in zs:
        lsp, lsn = _log2_sigmoids(z)
        if vis is not None:
            lsn = jnp.where(vis, lsn, 0.0)
        lsps.append(lsp)
        lsns.append(lsn)
    sums = _suffix_sums(lsns, uo)
    pvs, new_cars = [], []
    for lsp, (after, total), v, car in zip(lsps, sums, vs, cars):
        x = lsp + after
        if car is not None:
            cw = car.shape[1]
            x = x + (car[:, :tk] if cw >= tk else jnp.concatenate([car] * (tk // cw), axis=1))
        w = jnp.exp2(x)
        if vis is not None:
            w = jnp.where(vis, w, 0.0)
        pvs.append(jnp.dot(w.astype(_BF16), v, preferred_element_type=_F32))
        new_cars.append(total if car is None else car + total[:, :cw])
    return pvs, new_cars


def _head_norm_store(o_ref, acc_ref, g_ref, n_heads):
    for hs in _head_slices(n_heads):
        o_ref[:, hs] = _rms(acc_ref[:, hs], g_ref[:, hs]).astype(o_ref.dtype)


def _attn_prompt_kernel(q_ref, k_ref, v_ref, uo_ref, g_ref, *rest, n_heads, scale, cast_every):
    n_cast = len(cast_every)
    cast_in, o_ref, cast_out = rest[:n_cast], rest[n_cast], rest[n_cast + 1:2 * n_cast + 1]
    acc_ref, car_ref = rest[2 * n_cast + 1:]
    tq = ATTN_TILE
    i = pl.program_id(1)
    heads = _head_slices(n_heads)

    step = pl.program_id(0) * pl.num_programs(1) + i
    for src, dst, (every, n_blocks) in zip(cast_in, cast_out, cast_every):
        @pl.when(jnp.logical_and(step < every * n_blocks, lax.rem(step, every) == 0))
        def _(src=src, dst=dst):
            dst[...] = src[...].astype(_BF16)

    def tile(blk, n_blk, vis, first):
        k0 = pl.multiple_of(blk * tq, tq)
        qs = [q_ref[:, hs] for hs in heads]
        ks = [k_ref[pl.ds(k0, n_blk * tq), hs] for hs in heads]
        vs = [v_ref[pl.ds(k0, n_blk * tq), hs] for hs in heads]
        cars = [None if first else car_ref[:, hs] for hs in heads]
        pvs, new_cars = _sb_tile(qs, ks, vs, cars, uo_ref[...], vis, scale)
        for hs, pv, car in zip(heads, pvs, new_cars):
            if first:
                acc_ref[:, hs] = pv
            else:
                acc_ref[:, hs] += pv
            car_ref[:, hs] = car

    causal = (lax.broadcasted_iota(jnp.int32, (tq, tq), 1)
              < lax.broadcasted_iota(jnp.int32, (tq, tq), 0))
    tile(i, 1, causal, True)

    def cond(state):
        left, done = state
        return jnp.logical_and(left >= 2, done == 0)

    def body(state):
        left, _ = state
        tile(left - 2, 2, None, False)
        return left - 2, (jnp.max(car_ref[...]) < LOG2_WEIGHT_FLOOR).astype(jnp.int32)

    left, done = lax.while_loop(cond, body, (i, jnp.int32(0)))

    @pl.when(jnp.logical_and(left == 1, done == 0))
    def _():
        tile(0, 1, None, False)

    _head_norm_store(o_ref, acc_ref, g_ref, n_heads)


def _cast_plan(rows, n_steps):
    n_blocks = 1
    while n_blocks * 2 <= n_steps and rows % (n_blocks * 2) == 0 and (rows // (n_blocks * 2)) % 16 == 0:
        n_blocks *= 2
    every = 1
    while every * 2 * n_blocks <= n_steps:
        every *= 2
    return every, n_blocks


def _attn_prompt(pb, g_attn, layer, batch, tp, width, scale, cast):
    tq = ATTN_TILE
    nq = tp // tq
    n_heads = width // HEAD_DIM
    uo = _suffix_matrix(tq, tq)
    plans = [_cast_plan(w.shape[1], batch * nq) for w, _ in cast]

    def cast_specs(w, lyr, plan):
        every, n_blocks = plan
        rows, cols = w.shape[1:]
        blk = rows // n_blocks
        pick = lambda b, i: jnp.minimum((b * nq + i) // every, n_blocks - 1)
        return (pl.BlockSpec((None, blk, cols), lambda b, i: (lyr, pick(b, i), 0)),
                pl.BlockSpec((blk, cols), lambda b, i: (pick(b, i), 0)),
                jax.ShapeDtypeStruct((rows, cols), _BF16))

    specs = [cast_specs(w, lyr, plan) for (w, lyr), plan in zip(cast, plans)]
    outs = pl.pallas_call(
        functools.partial(_attn_prompt_kernel, n_heads=n_heads, scale=scale, cast_every=tuple(plans)),
        grid=(batch, nq),
        in_specs=[pl.BlockSpec((tq, width), lambda b, i: (b * nq + i, 0)),
                  pl.BlockSpec((tp, width), lambda b, i: (b, 1)),
                  pl.BlockSpec((tp, width), lambda b, i: (b, 2)),
                  pl.BlockSpec((2 * tq, 2 * tq), lambda b, i: (0, 0)),
                  pl.BlockSpec((None, 1, width), lambda b, i: (layer, 0, 0))] + [s[0] for s in specs],
        out_specs=[pl.BlockSpec((tq, width), lambda b, i: (b * nq + i, 0))] + [s[1] for s in specs],
        out_shape=[jax.ShapeDtypeStruct((batch * tp, width), _BF16)] + [s[2] for s in specs],
        scratch_shapes=[pltpu.VMEM((tq, width), _F32), pltpu.VMEM((tq, width), _F32)],
        compiler_params=_params(("arbitrary", "arbitrary")),
        name="attn_prompt",
    )(pb, pb, pb, uo, g_attn, *[w for w, _ in cast])
    return outs[0], outs[1:]


def _attn_sample_kernel(q_ref, kn_ref, vn_ref, ck_hbm, cv_hbm, uos_ref, uoc_ref, g_ref,
                        o_ref, acc_ref, car_ref, kbuf, vbuf, sem, *, n_heads, scale, ds, past, stream0):
    tk = CACHE_TILE
    n_ct = past // tk
    rows = tk * n_heads
    b = pl.program_id(0)
    heads = _head_slices(n_heads)

    def copies(stream, blk, slot):
        row0 = pl.multiple_of(((stream0 + stream) * past + blk * tk) * n_heads, rows)
        return (pltpu.make_async_copy(ck_hbm.at[pl.ds(row0, rows), :], kbuf.at[slot], sem.at[0, slot]),
                pltpu.make_async_copy(cv_hbm.at[pl.ds(row0, rows), :], vbuf.at[slot], sem.at[1, slot]))

    def start(stream, blk, slot):
        for c in copies(stream, blk, slot):
            c.start()

    def wait(stream, blk, slot):
        for c in copies(stream, blk, slot):
            c.wait()

    newest = n_ct - 1
    par = lax.rem(b, 2)

    @pl.when(b == 0)
    def _():
        start(b, newest, par)

    @pl.when(b + 1 < pl.num_programs(0))
    def _():
        start(b + 1, newest, 1 - par)

    causal = (lax.broadcasted_iota(jnp.int32, (ds, ds), 1)
              < lax.broadcasted_iota(jnp.int32, (ds, ds), 0))
    qs = [q_ref[:, hs] for hs in heads]
    pvs, cars = _sb_tile(qs, [kn_ref[:, hs] for hs in heads], [vn_ref[:, hs] for hs in heads],
                         [None] * n_heads, uos_ref[...], causal, scale)
    for h, hs in enumerate(heads):
        acc_ref[:, hs] = pvs[h]
        car_ref[h] = cars[h]

    def cache_tile(slot):
        kslot, vslot = kbuf.at[slot], vbuf.at[slot]
        ks = [kslot[pl.ds(h, tk, stride=n_heads), :].astype(_BF16) for h in range(n_heads)]
        vs = [vslot[pl.ds(h, tk, stride=n_heads), :].astype(_BF16) for h in range(n_heads)]
        pvs, cars = _sb_tile([q_ref[:, hs] for hs in heads], ks, vs,
                             [car_ref[h] for h in range(n_heads)], uoc_ref[...], None, scale)
        for h, hs in enumerate(heads):
            acc_ref[:, hs] += pvs[h]
            car_ref[h] = cars[h]
        return (jnp.max(car_ref[...]) < LOG2_WEIGHT_FLOOR).astype(jnp.int32)

    wait(b, newest, par)
    done = cache_tile(par)

    if n_ct > 1:
        older = lambda blk: 2 + lax.rem(blk, 2)

        @pl.when(done == 0)
        def _():
            start(b, newest - 1, older(newest - 1))

        def cond(state):
            blk, done = state
            return jnp.logical_and(blk >= 0, done == 0)

        def body(state):
            blk, _ = state
            wait(b, blk, older(blk))

            @pl.when(blk > 0)
            def _():
                start(b, blk - 1, older(blk - 1))

            return blk - 1, cache_tile(older(blk))

        blk_end, _ = lax.while_loop(cond, body, (jnp.int32(newest - 1), done))

        @pl.when(jnp.logical_and(blk_end >= 0, blk_end < newest - 1))
        def _():
            wait(b, blk_end, older(blk_end))

    _head_norm_store(o_ref, acc_ref, g_ref, n_heads)


def _attn_sample(pb, cache_k, cache_v, g_attn, layer, row0, n_streams, ds, past, width, scale):
    tk = CACHE_TILE
    n_heads = width // HEAD_DIM
    blk0 = row0 // ds
    uos = _suffix_matrix(ds, tk)
    uoc = _suffix_matrix(tk, tk)
    return pl.pallas_call(
        functools.partial(_attn_sample_kernel, n_heads=n_heads, scale=scale, ds=ds, past=past,
                          stream0=layer * n_streams),
        grid=(n_streams,),
        in_specs=[pl.BlockSpec((ds, width), lambda b: (blk0 + b, 0)),
                  pl.BlockSpec((ds, width), lambda b: (blk0 + b, 1)),
                  pl.BlockSpec((ds, width), lambda b: (blk0 + b, 2)),
                  pl.BlockSpec(memory_space=pl.ANY),
                  pl.BlockSpec(memory_space=pl.ANY),
                  pl.BlockSpec((2 * ds, 2 * tk), lambda b: (0, 0)),
                  pl.BlockSpec((2 * tk, 2 * tk), lambda b: (0, 0)),
                  pl.BlockSpec((None, 1, width), lambda b: (layer, 0, 0))],
        out_specs=pl.BlockSpec((ds, width), lambda b: (b, 0)),
        out_shape=jax.ShapeDtypeStruct((n_streams * ds, width), _BF16),
        scratch_shapes=[pltpu.VMEM((ds, width), _F32),
                        pltpu.VMEM((n_heads, ds, tk), _F32),
                        pltpu.VMEM((4, tk * n_heads, HEAD_DIM), _F32),
                        pltpu.VMEM((4, tk * n_heads, HEAD_DIM), _F32),
                        pltpu.SemaphoreType.DMA((2, 4))],
        compiler_params=_params(("arbitrary",)),
        name="attn_sample",
    )(pb, pb, pb, cache_k, cache_v, uos, uoc, g_attn)


def _conv_kernel(gb_ref, gc_ref, ci_ref, s0_ref, s1_ref, w_ref, g_ref, o_ref, u_ref, ubuf_ref,
                 *, tp, n_prompt, ds, width):
    tc = CONV_TILE
    i = pl.program_id(0)

    @pl.when(i == 0)
    def _():
        ubuf_ref[0:8, :] = jnp.zeros((8, width), _F32)

    u = gc_ref[...] * ci_ref[...]
    u_ref[...] = u
    ubuf_ref[8:8 + tc, :] = u
    u1 = ubuf_ref[7:7 + tc, :]
    u2 = ubuf_ref[6:6 + tc, :]

    row = i * tc + lax.broadcasted_iota(jnp.int32, (tc, 1), 0)
    is_sample = row >= n_prompt
    pos = jnp.where(is_sample, lax.rem(row - n_prompt, ds), lax.rem(row, tp))
    first = pos == 0
    second = pos == 1
    seqs = tc // ds
    zero = jnp.zeros((tc, width), _F32)
    st0 = jnp.broadcast_to(s0_ref[...][:, None, :], (seqs, ds, width)).reshape(tc, width)
    st1 = jnp.broadcast_to(s1_ref[...][:, None, :], (seqs, ds, width)).reshape(tc, width)
    st0 = jnp.where(is_sample, st0, zero)
    st1 = jnp.where(is_sample, st1, zero)
    u1 = jnp.where(first, st1, u1)
    u2 = jnp.where(first, st0, jnp.where(second, st1, u2))
    y = u2 * w_ref[0:1, :] + u1 * w_ref[1:2, :] + u * w_ref[2:3, :]
    c = gb_ref[...] * y
    for gs in _head_slices(width // HEAD_DIM):
        o_ref[:, gs] = _rms(c[:, gs], g_ref[:, gs]).astype(o_ref.dtype)
    ubuf_ref[0:8, :] = u[tc - 8:tc, :]


def _conv(p, s0, s1, conv_w, g_conv, layer, tp, n_prompt, ds, width):
    tc = CONV_TILE
    n = p.shape[0]
    seqs = tc // ds
    first_sample_tile = n_prompt // tc

    def state_map(i):
        return (jnp.maximum(i - first_sample_tile, 0), 0)

    return pl.pallas_call(
        functools.partial(_conv_kernel, tp=tp, n_prompt=n_prompt, ds=ds, width=width),
        grid=(n // tc,),
        in_specs=[pl.BlockSpec((tc, width), lambda i: (i, 0)),
                  pl.BlockSpec((tc, width), lambda i: (i, 1)),
                  pl.BlockSpec((tc, width), lambda i: (i, 2)),
                  pl.BlockSpec((seqs, width), state_map),
                  pl.BlockSpec((seqs, width), state_map),
                  pl.BlockSpec((None, CONV_K, width), lambda i: (layer, 0, 0)),
                  pl.BlockSpec((None, 1, width), lambda i: (layer, 0, 0))],
        out_specs=[pl.BlockSpec((tc, width), lambda i: (i, 0)),
                   pl.BlockSpec((tc, width), lambda i: (i, 0))],
        out_shape=[jax.ShapeDtypeStruct((n, width), _BF16),
                   jax.ShapeDtypeStruct((n, width), _F32)],
        scratch_shapes=[pltpu.VMEM((8 + tc, width), _F32)],
        compiler_params=_params(("arbitrary",)),
        name="gated_conv",
    )(p, p, p, s0, s1, conv_w, g_conv)


def _outproj_kernel(ap_ref, as_ref, c_ref, w_ref, h_ref, gpost_ref, gnext_ref, ho_ref, xo_ref, a_buf,
                    *, width, n_prompt, n_tiles):
    i = pl.program_id(0)
    tm = a_buf.shape[0]
    first_mixed = n_prompt // tm

    @pl.when(i < first_mixed)
    def _():
        a_buf[...] = ap_ref[...]

    for t in range(first_mixed, n_tiles):
        p_rows = max(0, n_prompt - t * tm)
        s0 = t * tm + p_rows - n_prompt

        @pl.when(i == t)
        def _(p_rows=p_rows, s0=s0):
            if p_rows:
                a_buf[0:p_rows, :] = ap_ref[0:p_rows, :]
            a_buf[p_rows:tm, :] = as_ref[s0:s0 + tm - p_rows, :]

    y = (jnp.dot(a_buf[...], w_ref[0:width, :], preferred_element_type=_F32)
         + jnp.dot(c_ref[...], w_ref[width:2 * width, :], preferred_element_type=_F32))
    h_new = h_ref[...] + _rms(y, gpost_ref[...])
    ho_ref[...] = h_new
    xo_ref[...] = _rms(h_new, gnext_ref[...]).astype(xo_ref.dtype)


def _outproj(attn_p, attn_s, conv_n, w_out, h, g_post, g_next, layer, tm):
    n, d = h.shape
    n_prompt, width = attn_p.shape
    n_tiles = n // tm
    last_prompt_tile = (n_prompt - 1) // tm
    assert n_prompt % 16 == 0 and tm % 16 == 0
    gain = pl.BlockSpec((None, 1, d), lambda i: (layer, 0, 0))
    return pl.pallas_call(
        functools.partial(_outproj_kernel, width=width, n_prompt=n_prompt, n_tiles=n_tiles),
        grid=(n_tiles,),
        in_specs=[pl.BlockSpec((tm, width), lambda i: (jnp.minimum(i, last_prompt_tile), 0)),
                  pl.BlockSpec(attn_s.shape, lambda i: (0, 0)),
                  pl.BlockSpec((tm, width), lambda i: (i, 0)),
                  pl.BlockSpec((2 * width, d), lambda i: (0, 0)),
                  pl.BlockSpec((tm, d), lambda i: (i, 0)),
                  gain, gain],
        out_specs=[pl.BlockSpec((tm, d), lambda i: (i, 0)),
                   pl.BlockSpec((tm, d), lambda i: (i, 0))],
        out_shape=[jax.ShapeDtypeStruct((n, d), _F32),
                   jax.ShapeDtypeStruct((n, d), _BF16)],
        scratch_shapes=[pltpu.VMEM((tm, width), _BF16)],
        compiler_params=_params(("arbitrary",)),
        name="out_proj",
    )(attn_p, attn_s, conv_n, w_out, h, g_post, g_next)


def _ffn_kernel(x_ref, wg_ref, wu_ref, wd_ref, h_ref, gpost_ref, gnext_ref, ho_ref, xo_ref):
    c = pl.program_id(1)

    @pl.when(c == 0)
    def _():
        ho_ref[...] = jnp.zeros_like(ho_ref)

    tm = x_ref.shape[0]
    split = (tm // 2 + 15) // 16 * 16
    groups = [slice(0, split), slice(split, tm)]
    wg, wu, wd = wg_ref[...], wu_ref[...], wd_ref[...]
    gate_up = [(jnp.dot(x_ref[rs, :], wg, preferred_element_type=_F32),
                jnp.dot(x_ref[rs, :], wu, preferred_element_type=_F32)) for rs in groups]
    for rs, (gate, up) in zip(groups, gate_up):
        act = (gate * jax.nn.sigmoid(gate) * up).astype(_BF16)
        ho_ref[rs, :] += jnp.dot(act, wd, preferred_element_type=_F32)

    @pl.when(c == pl.num_programs(1) - 1)
    def _():
        h_new = h_ref[...] + _rms(ho_ref[...], gpost_ref[...])
        ho_ref[...] = h_new
        xo_ref[...] = _rms(h_new, gnext_ref[...]).astype(xo_ref.dtype)


def _ffn(xf, w_gate, w_up, w_down, h, g_post, g_next, layer, next_layer, tm):
    n, d = h.shape
    d_ff = w_gate.shape[1]
    tf = FF_TILE
    return pl.pallas_call(
        _ffn_kernel,
        grid=(n // tm, d_ff // tf),
        in_specs=[pl.BlockSpec((tm, d), lambda i, c: (i, 0)),
                  pl.BlockSpec((d, tf), lambda i, c: (0, c)),
                  pl.BlockSpec((d, tf), lambda i, c: (0, c)),
                  pl.BlockSpec((tf, d), lambda i, c: (c, 0)),
                  pl.BlockSpec((tm, d), lambda i, c: (i, 0)),
                  pl.BlockSpec((None, 1, d), lambda i, c: (layer, 0, 0)),
                  pl.BlockSpec((None, 1, d), lambda i, c: (next_layer, 0, 0))],
        out_specs=[pl.BlockSpec((tm, d), lambda i, c: (i, 0)),
                   pl.BlockSpec((tm, d), lambda i, c: (i, 0))],
        out_shape=[jax.ShapeDtypeStruct((n, d), _F32),
                   jax.ShapeDtypeStruct((n, d), _BF16)],
        compiler_params=_params(("arbitrary", "arbitrary")),
        name="swiglu_ffn",
    )(xf, w_gate, w_up, w_down, h, g_post, g_next)


def kernel(x_prompt, x_sample, cache_k, cache_v, state_conv, meta, w_in, w_out, conv_w, g_pre_mix, g_post_mix,
           g_attn_out, g_conv_out, g_pre_ffn, g_post_ffn, w_gate, w_up, w_down):
    batch, seq, d = x_prompt.shape
    n_streams, ds, _ = x_sample.shape
    depth = w_in.shape[0]
    width = w_out.shape[1] // 2
    n_heads = width // HEAD_DIM
    n_meta = meta.shape[0]
    past = cache_k.shape[2]
    t_real = n_meta + seq
    tp = -(-t_real // ATTN_TILE) * ATTN_TILE
    n_prompt = batch * tp
    n_sample = n_streams * ds
    n_tok = n_prompt + n_sample
    assert n_prompt % CONV_TILE == 0 and n_sample % CONV_TILE == 0 and CONV_TILE % ds == 0
    assert tp % ds == 0 and past % CACHE_TILE == 0 and ds >= CONV_K - 1
    assert w_gate.shape[2] % FF_TILE == 0 and cache_k.shape[3:] == (n_heads, HEAD_DIM)
    tm = _pick_tile(n_tok, ROW_TILE_TARGET)
    tm_in = _pick_tile(n_tok, IN_ROW_TILE_TARGET)
    scale = HEAD_DIM ** -0.5

    pad = jnp.zeros((tp - t_real, d), _F32)
    pieces = [piece for b in range(batch) for piece in (meta.astype(_F32), x_prompt[b], pad)]
    h = jnp.concatenate(pieces + [x_sample.reshape(n_sample, d)], axis=0)

    rows = lambda a: a.reshape(depth, 1, -1)
    g_pre_mix, g_post_mix, g_pre_ffn, g_post_ffn = rows(g_pre_mix), rows(g_post_mix), rows(g_pre_ffn), rows(g_post_ffn)
    g_attn_out, g_conv_out = rows(g_attn_out), rows(g_conv_out)
    cache_k2 = cache_k.reshape(-1, HEAD_DIM)
    cache_v2 = cache_v.reshape(-1, HEAD_DIM)

    xn = _norm(h, g_pre_mix, 0, tm)
    w_in_b = w_in[0].astype(_BF16)
    k_l, v_l, u_l = [], [], []
    for l in range(depth):
        p, pb, k3, v3 = _inproj(xn, w_in_b, tm_in, width)
        cast = [(w_out, l), (w_gate, l), (w_up, l), (w_down, l)] + ([(w_in, l + 1)] if l + 1 < depth else [])
        attn_p, casted = _attn_prompt(pb, g_attn_out, l, batch, tp, width, scale, cast)
        w_out_b, w_gate_b, w_up_b, w_down_b = casted[:4]
        w_in_b = casted[4] if l + 1 < depth else None
        attn_s = _attn_sample(pb, cache_k2, cache_v2, g_attn_out, l, n_prompt, n_streams, ds, past, width, scale)
        conv_n, u = _conv(p, state_conv[l, :, 0, :], state_conv[l, :, 1, :], conv_w, g_conv_out, l,
                          tp, n_prompt, ds, width)
        h, xf = _outproj(attn_p, attn_s, conv_n, w_out_b, h, g_post_mix, g_pre_ffn, l, tm)
        h, xn = _ffn(xf, w_gate_b, w_up_b, w_down_b, h, g_post_ffn, g_pre_mix, l, min(l + 1, depth - 1), tm)
        k_l.append(k3.reshape(n_tok, n_heads, HEAD_DIM))
        v_l.append(v3.reshape(n_tok, n_heads, HEAD_DIM))
        u_l.append(u)

    def prompt_part(xs, lo, hi):
        return jnp.stack([jnp.stack([x[b * tp + lo:b * tp + hi] for b in range(batch)]) for x in xs])

    def sample_part(xs, lo):
        return jnp.stack([x[n_prompt:].reshape((n_streams, ds) + x.shape[1:])[:, lo:] for x in xs])

    y_prompt = jnp.stack([h[b * tp + n_meta:b * tp + t_real] for b in range(batch)])
    y_sample = h[n_prompt:].reshape(n_streams, ds, d)
    return (y_prompt, y_sample,
            prompt_part(k_l, 0, t_real), prompt_part(v_l, 0, t_real), prompt_part(u_l, t_real - (CONV_K - 1), t_real),
            sample_part(k_l, 0), sample_part(v_l, 0), sample_part(u_l, ds - (CONV_K - 1)))
```

```python
import functools

import jax
import jax.numpy as jnp
from jax import lax
from jax.experimental import pallas as pl
from jax.experimental.pallas import tpu as pltpu

HEAD_DIM = 128
CONV_K = 3
EPS = 1e-6
ATTN_TILE = 128
CACHE_TILE = 256
CONV_TILE = 256
ROW_TILE_TARGET = 600
IN_ROW_TILE_TARGET = 1200
IN_COL_TILE = 512
FF_TILE = 512
VMEM_LIMIT = 56 * 1024 * 1024
LOG2_E = 1.4426950408889634
LOG2_WEIGHT_FLOOR = -152.0

_F32 = jnp.float32
_BF16 = jnp.bfloat16
_NT = (((1,), (1,)), ((), ()))


def _pick_tile(n, target, mult=16):
    best = None
    for t in range(mult, target + 1, mult):
        if n % t == 0:
            best = t
    assert best is not None, (n, target)
    return best


def _params(sem):
    return pltpu.CompilerParams(dimension_semantics=sem, vmem_limit_bytes=VMEM_LIMIT)


def _rms(x, g):
    ms = jnp.mean(x * x, axis=-1, keepdims=True)
    return x * lax.rsqrt(ms + EPS) * g


def _head_slices(n_heads):
    return [slice(h * HEAD_DIM, (h + 1) * HEAD_DIM) for h in range(n_heads)]


def _norm_kernel(h_ref, g_ref, o_ref):
    o_ref[...] = _rms(h_ref[...], g_ref[...]).astype(o_ref.dtype)


def _norm(h, g, layer, tm):
    n, d = h.shape
    return pl.pallas_call(
        _norm_kernel,
        grid=(n // tm,),
        in_specs=[pl.BlockSpec((tm, d), lambda i: (i, 0)),
                  pl.BlockSpec((None, 1, d), lambda i: (layer, 0, 0))],
        out_specs=pl.BlockSpec((tm, d), lambda i: (i, 0)),
        out_shape=jax.ShapeDtypeStruct((n, d), _BF16),
        compiler_params=_params(("arbitrary",)),
        name="pre_norm",
    )(h, g)


def _inproj_kernel(x_ref, w_ref, p_ref, pb_ref, k_ref, v_ref, *, n_heads, tm, wpk):
    j = pl.program_id(1)
    acc = jnp.dot(x_ref[...], w_ref[...], preferred_element_type=_F32)
    p_ref[...] = acc
    pb_ref[...] = acc.astype(_BF16)

    heads_per_tile = n_heads // wpk
    for ref, first in ((k_ref, wpk), (v_ref, 2 * wpk)):
        for part in range(wpk):
            @pl.when(j == first + part)
            def _(ref=ref, part=part):
                for hh, hs in enumerate(_head_slices(heads_per_tile)):
                    ref[pl.ds(part * heads_per_tile + hh, tm, stride=n_heads), :] = p_ref[:, hs]


def _inproj(xn, w, tm, width):
    n, d = xn.shape
    n_heads = width // HEAD_DIM
    tn = min(IN_COL_TILE, width)
    wpk = width // tn
    kv_shape = jax.ShapeDtypeStruct((n * n_heads, HEAD_DIM), _F32)
    kv_spec = pl.BlockSpec((tm * n_heads, HEAD_DIM), lambda i, j: (i, 0))
    return pl.pallas_call(
        functools.partial(_inproj_kernel, n_heads=n_heads, tm=tm, wpk=wpk),
        grid=(n // tm, 6 * wpk),
        in_specs=[pl.BlockSpec((tm, d), lambda i, j: (i, 0)),
                  pl.BlockSpec((d, tn), lambda i, j: (0, j))],
        out_specs=[pl.BlockSpec((tm, tn), lambda i, j: (i, jnp.where(j < 3 * wpk, 3 * wpk, j - 3 * wpk))),
                   pl.BlockSpec((tm, tn), lambda i, j: (i, jnp.minimum(j, 3 * wpk))),
                   kv_spec, kv_spec],
        out_shape=[jax.ShapeDtypeStruct((n, 3 * width + tn), _F32),
                   jax.ShapeDtypeStruct((n, 3 * width + tn), _BF16),
                   kv_shape, kv_shape],
        compiler_params=_params(("arbitrary", "arbitrary")),
        name="in_proj",
    )(xn, w)


def _log2_sigmoids(z2):
    lsp = jnp.minimum(z2, 0.0) - jnp.log2(1.0 + jnp.exp2(-jnp.abs(z2)))
    return lsp, lsp - z2


def _suffix_sums(l_mats, uo):
    tq, tk = l_mats[0].shape
    sub, wc = uo.shape[0] // 2, uo.shape[1] // 2
    his = [l.astype(_BF16) for l in l_mats]
    los = [(l - hi.astype(_F32)).astype(_BF16) for l, hi in zip(l_mats, his)]
    if sub % HEAD_DIM:
        assert tk == sub
        top = uo[:tk]
        rs = [jnp.dot(hi, top, preferred_element_type=_F32) + jnp.dot(lo, top, preferred_element_type=_F32)
              for hi, lo in zip(his, los)]
        return [(r[:, :tk], r[:, wc:]) for r in rs]
    n_sub = tk // sub
    cols = [slice(b * sub, (b + 1) * sub) for b in range(n_sub)]
    lhs = jnp.concatenate([jnp.concatenate([hi[:, cs], lo[:, cs]], axis=1)
                           for hi, lo in zip(his, los) for cs in cols], axis=0)
    r = jnp.dot(lhs, uo, preferred_element_type=_F32)
    out = []
    for h in range(len(l_mats)):
        newer, afters = None, [None] * n_sub
        for b in reversed(range(n_sub)):
            r_hb = r[(h * n_sub + b) * tq:(h * n_sub + b + 1) * tq]
            afters[b] = r_hb[:, :sub] if newer is None else r_hb[:, :sub] + newer[:, :sub]
            newer = r_hb[:, wc:] if newer is None else newer + r_hb[:, wc:]
        out.append((afters[0] if n_sub == 1 else jnp.concatenate(afters, axis=1), newer))
    return out


def _suffix_matrix(rows, width):
    j = lax.rem(lax.broadcasted_iota(jnp.int32, (2 * rows, 2 * width), 0), rows)
    s = lax.broadcasted_iota(jnp.int32, (2 * rows, 2 * width), 1)
    return jnp.where((s >= width) | (j > s), 1.0, 0.0).astype(_BF16)


def _sb_tile(qs, ks, vs, cars, uo, vis, scale):
    tk = ks[0].shape[0]
    zs = [lax.dot_general(q, k, _NT, preferred_element_type=_F32) * (scale * LOG2_E) for q, k in zip(qs, ks)]
    lsps, lsns = [], []
    for z in zs:
        lsp, lsn = _log2_sigmoids(z)
        if vis is not None:
            lsn = jnp.where(vis, lsn, 0.0)
        lsps.append(lsp)
        lsns.append(lsn)
    sums = _suffix_sums(lsns, uo)
    pvs, new_cars = [], []
    for lsp, (after, total), v, car in zip(lsps, sums, vs, cars):
        x = lsp + after
        if car is not None:
            cw = car.shape[1]
            x = x + (car[:, :tk] if cw >= tk else jnp.concatenate([car] * (tk // cw), axis=1))
        w = jnp.exp2(x)
        if vis is not None:
            w = jnp.where(vis, w, 0.0)
        pvs.append(jnp.dot(w.astype(_BF16), v, preferred_element_type=_F32))
        new_cars.append(total if car is None else car + total[:, :cw])
    return pvs, new_cars


def _head_norm_store(o_ref, acc_ref, g_ref, n_heads):
    for hs in _head_slices(n_heads):
        o_ref[:, hs] = _rms(acc_ref[:, hs], g_ref[:, hs]).astype(o_ref.dtype)


def _attn_prompt_kernel(q_ref, k_ref, v_ref, uo_ref, g_ref, *rest, n_heads, scale, cast_every):
    n_cast = len(cast_every)
    cast_in, o_ref, cast_out = rest[:n_cast], rest[n_cast], rest[n_cast + 1:2 * n_cast + 1]
    acc_ref, car_ref = rest[2 * n_cast + 1:]
    tq = ATTN_TILE
    i = pl.program_id(1)
    heads = _head_slices(n_heads)

    step = pl.program_id(0) * pl.num_programs(1) + i
    for src, dst, (every, n_blocks) in zip(cast_in, cast_out, cast_every):
        @pl.when(jnp.logical_and(step < every * n_blocks, lax.rem(step, every) == 0))
        def _(src=src, dst=dst):
            dst[...] = src[...].astype(_BF16)

    def tile(blk, n_blk, vis, first):
        k0 = pl.multiple_of(blk * tq, tq)
        qs = [q_ref[:, hs] for hs in heads]
        ks = [k_ref[pl.ds(k0, n_blk * tq), hs] for hs in heads]
        vs = [v_ref[pl.ds(k0, n_blk * tq), hs] for hs in heads]
        cars = [None if first else car_ref[:, hs] for hs in heads]
        pvs, new_cars = _sb_tile(qs, ks, vs, cars, uo_ref[...], vis, scale)
        for hs, pv, car in zip(heads, pvs, new_cars):
            if first:
                acc_ref[:, hs] = pv
            else:
                acc_ref[:, hs] += pv
            car_ref[:, hs] = car

    def causal_last(n_blk):
        off = (n_blk - 1) * tq
        return (lax.broadcasted_iota(jnp.int32, (tq, n_blk * tq), 1) - off
                < lax.broadcasted_iota(jnp.int32, (tq, n_blk * tq), 0))

    @pl.when(i >= 2)
    def _():
        tile(i - 2, 3, causal_last(3), True)

    @pl.when(i < 2)
    def _():
        tile(i, 1, causal_last(1), True)

    def all_rows_done():
        return (jnp.max(car_ref[...]) < LOG2_WEIGHT_FLOOR).astype(jnp.int32)

    def cond(state):
        left, done = state
        return jnp.logical_and(left >= 2, done == 0)

    def body(state):
        left, _ = state
        tile(left - 2, 2, None, False)
        return left - 2, all_rows_done()

    left, done = lax.while_loop(cond, body, (jnp.where(i >= 2, i - 2, i), all_rows_done()))

    @pl.when(jnp.logical_and(left == 1, done == 0))
    def _():
        tile(0, 1, None, False)

    _head_norm_store(o_ref, acc_ref, g_ref, n_heads)


def _cast_plan(rows, n_steps):
    n_blocks = 1
    while n_blocks * 2 <= n_steps and rows % (n_blocks * 2) == 0 and (rows // (n_blocks * 2)) % 16 == 0:
        n_blocks *= 2
    every = 1
    while every * 2 * n_blocks <= n_steps:
        every *= 2
    return every, n_blocks


def _attn_prompt(pb, g_attn, layer, batch, tp, width, scale, cast):
    tq = ATTN_TILE
    nq = tp // tq
    n_heads = width // HEAD_DIM
    uo = _suffix_matrix(tq, tq)
    plans = [_cast_plan(w.shape[1], batch * nq) for w, _ in cast]

    def cast_specs(w, lyr, plan):
        every, n_blocks = plan
        rows, cols = w.shape[1:]
        blk = rows // n_blocks
        pick = lambda b, i: jnp.minimum((b * nq + i) // every, n_blocks - 1)
        return (pl.BlockSpec((None, blk, cols), lambda b, i: (lyr, pick(b, i), 0)),
                pl.BlockSpec((blk, cols), lambda b, i: (pick(b, i), 0)),
                jax.ShapeDtypeStruct((rows, cols), _BF16))

    specs = [cast_specs(w, lyr, plan) for (w, lyr), plan in zip(cast, plans)]
    outs = pl.pallas_call(
        functools.partial(_attn_prompt_kernel, n_heads=n_heads, scale=scale, cast_every=tuple(plans)),
        grid=(batch, nq),
        in_specs=[pl.BlockSpec((tq, width), lambda b, i: (b * nq + i, 0)),
                  pl.BlockSpec((tp, width), lambda b, i: (b, 1)),
                  pl.BlockSpec((tp, width), lambda b, i: (b, 2)),
                  pl.BlockSpec((2 * tq, 2 * tq), lambda b, i: (0, 0)),
                  pl.BlockSpec((None, 1, width), lambda b, i: (layer, 0, 0))] + [s[0] for s in specs],
        out_specs=[pl.BlockSpec((tq, width), lambda b, i: (b * nq + i, 0))] + [s[1] for s in specs],
        out_shape=[jax.ShapeDtypeStruct((batch * tp, width), _BF16)] + [s[2] for s in specs],
        scratch_shapes=[pltpu.VMEM((tq, width), _F32), pltpu.VMEM((tq, width), _F32)],
        compiler_params=_params(("arbitrary", "arbitrary")),
        name="attn_prompt",
    )(pb, pb, pb, uo, g_attn, *[w for w, _ in cast])
    return outs[0], outs[1:]


def _attn_sample_kernel(q_ref, kn_ref, vn_ref, ck_hbm, cv_hbm, uos_ref, uoc_ref, g_ref,
                        o_ref, acc_ref, car_ref, kbuf, vbuf, sem, *, n_heads, scale, ds, past, stream0):
    tk = CACHE_TILE
    n_ct = past // tk
    rows = tk * n_heads
    b = pl.program_id(0)
    heads = _head_slices(n_heads)

    def copies(stream, blk, slot):
        row0 = pl.multiple_of(((stream0 + stream) * past + blk * tk) * n_heads, rows)
        return (pltpu.make_async_copy(ck_hbm.at[pl.ds(row0, rows), :], kbuf.at[slot], sem.at[0, slot]),
                pltpu.make_async_copy(cv_hbm.at[pl.ds(row0, rows), :], vbuf.at[slot], sem.at[1, slot]))

    def start(stream, blk, slot):
        for c in copies(stream, blk, slot):
            c.start()

    def wait(stream, blk, slot):
        for c in copies(stream, blk, slot):
            c.wait()

    newest = n_ct - 1
    par = lax.rem(b, 2)

    @pl.when(b == 0)
    def _():
        start(b, newest, par)

    @pl.when(b + 1 < pl.num_programs(0))
    def _():
        start(b + 1, newest, 1 - par)

    causal = (lax.broadcasted_iota(jnp.int32, (ds, ds), 1)
              < lax.broadcasted_iota(jnp.int32, (ds, ds), 0))
    qs = [q_ref[:, hs] for hs in heads]
    pvs, cars = _sb_tile(qs, [kn_ref[:, hs] for hs in heads], [vn_ref[:, hs] for hs in heads],
                         [None] * n_heads, uos_ref[...], causal, scale)
    for h, hs in enumerate(heads):
        acc_ref[:, hs] = pvs[h]
        car_ref[h] = cars[h]

    def cache_tile(slot):
        kslot, vslot = kbuf.at[slot], vbuf.at[slot]
        ks = [kslot[pl.ds(h, tk, stride=n_heads), :].astype(_BF16) for h in range(n_heads)]
        vs = [vslot[pl.ds(h, tk, stride=n_heads), :].astype(_BF16) for h in range(n_heads)]
        pvs, cars = _sb_tile([q_ref[:, hs] for hs in heads], ks, vs,
                             [car_ref[h] for h in range(n_heads)], uoc_ref[...], None, scale)
        for h, hs in enumerate(heads):
            acc_ref[:, hs] += pvs[h]
            car_ref[h] = cars[h]
        return (jnp.max(car_ref[...]) < LOG2_WEIGHT_FLOOR).astype(jnp.int32)

    wait(b, newest, par)
    done = cache_tile(par)

    if n_ct > 1:
        older = lambda blk: 2 + lax.rem(blk, 2)

        @pl.when(done == 0)
        def _():
            start(b, newest - 1, older(newest - 1))

        def cond(state):
            blk, done = state
            return jnp.logical_and(blk >= 0, done == 0)

        def body(state):
            blk, _ = state
            wait(b, blk, older(blk))

            @pl.when(blk > 0)
            def _():
                start(b, blk - 1, older(blk - 1))

            return blk - 1, cache_tile(older(blk))

        blk_end, _ = lax.while_loop(cond, body, (jnp.int32(newest - 1), done))

        @pl.when(jnp.logical_and(blk_end >= 0, blk_end < newest - 1))
        def _():
            wait(b, blk_end, older(blk_end))

    _head_norm_store(o_ref, acc_ref, g_ref, n_heads)


def _attn_sample(pb, cache_k, cache_v, g_attn, layer, row0, n_streams, ds, past, width, scale):
    tk = CACHE_TILE
    n_heads = width // HEAD_DIM
    blk0 = row0 // ds
    uos = _suffix_matrix(ds, tk)
    uoc = _suffix_matrix(tk, tk)
    return pl.pallas_call(
        functools.partial(_attn_sample_kernel, n_heads=n_heads, scale=scale, ds=ds, past=past,
                          stream0=layer * n_streams),
        grid=(n_streams,),
        in_specs=[pl.BlockSpec((ds, width), lambda b: (blk0 + b, 0)),
                  pl.BlockSpec((ds, width), lambda b: (blk0 + b, 1)),
                  pl.BlockSpec((ds, width), lambda b: (blk0 + b, 2)),
                  pl.BlockSpec(memory_space=pl.ANY),
                  pl.BlockSpec(memory_space=pl.ANY),
                  pl.BlockSpec((2 * ds, 2 * tk), lambda b: (0, 0)),
                  pl.BlockSpec((2 * tk, 2 * tk), lambda b: (0, 0)),
                  pl.BlockSpec((None, 1, width), lambda b: (layer, 0, 0))],
        out_specs=pl.BlockSpec((ds, width), lambda b: (b, 0)),
        out_shape=jax.ShapeDtypeStruct((n_streams * ds, width), _BF16),
        scratch_shapes=[pltpu.VMEM((ds, width), _F32),
                        pltpu.VMEM((n_heads, ds, tk), _F32),
                        pltpu.VMEM((4, tk * n_heads, HEAD_DIM), _F32),
                        pltpu.VMEM((4, tk * n_heads, HEAD_DIM), _F32),
                        pltpu.SemaphoreType.DMA((2, 4))],
        compiler_params=_params(("arbitrary",)),
        name="attn_sample",
    )(pb, pb, pb, cache_k, cache_v, uos, uoc, g_attn)


def _conv_kernel(gb_ref, gc_ref, ci_ref, s0_ref, s1_ref, w_ref, g_ref, o_ref, tail_p_ref, tail_s_ref, ubuf_ref,
                 *, tp, t_real, n_prompt, ds, width):
    tc = CONV_TILE
    i = pl.program_id(0)

    @pl.when(i == 0)
    def _():
        ubuf_ref[0:8, :] = jnp.zeros((8, width), _F32)

    u = gc_ref[...] * ci_ref[...]
    ubuf_ref[8:8 + tc, :] = u

    for b in range(n_prompt // tp):
        g0 = b * tp + t_real - 8

        @pl.when(i == g0 // tc)
        def _(b=b, g0=g0):
            tail_p_ref[8 * b:8 * b + 8, :] = u[g0 % tc:g0 % tc + 8, :]

    @pl.when(i >= n_prompt // tc)
    def _():
        for s in range(tc // ds):
            tail_s_ref[8 * s:8 * s + 8, :] = u[ds * s + ds - 8:ds * s + ds, :]

    u1 = ubuf_ref[7:7 + tc, :]
    u2 = ubuf_ref[6:6 + tc, :]

    row = i * tc + lax.broadcasted_iota(jnp.int32, (tc, 1), 0)
    is_sample = row >= n_prompt
    pos = jnp.where(is_sample, lax.rem(row - n_prompt, ds), lax.rem(row, tp))
    first = pos == 0
    second = pos == 1
    seqs = tc // ds
    zero = jnp.zeros((tc, width), _F32)
    st0 = jnp.broadcast_to(s0_ref[...][:, None, :], (seqs, ds, width)).reshape(tc, width)
    st1 = jnp.broadcast_to(s1_ref[...][:, None, :], (seqs, ds, width)).reshape(tc, width)
    st0 = jnp.where(is_sample, st0, zero)
    st1 = jnp.where(is_sample, st1, zero)
    u1 = jnp.where(first, st1, u1)
    u2 = jnp.where(first, st0, jnp.where(second, st1, u2))
    y = u2 * w_ref[0:1, :] + u1 * w_ref[1:2, :] + u * w_ref[2:3, :]
    c = gb_ref[...] * y
    for gs in _head_slices(width // HEAD_DIM):
        o_ref[:, gs] = _rms(c[:, gs], g_ref[:, gs]).astype(o_ref.dtype)
    ubuf_ref[0:8, :] = u[tc - 8:tc, :]


def _conv(p, s0, s1, conv_w, g_conv, layer, tp, t_real, n_prompt, ds, width):
    tc = CONV_TILE
    n = p.shape[0]
    seqs = tc // ds
    first_sample_tile = n_prompt // tc
    batch = n_prompt // tp
    n_streams = (n - n_prompt) // ds
    assert t_real % 8 == 0 and ds % 8 == 0

    def state_map(i):
        return (jnp.maximum(i - first_sample_tile, 0), 0)

    return pl.pallas_call(
        functools.partial(_conv_kernel, tp=tp, t_real=t_real, n_prompt=n_prompt, ds=ds, width=width),
        grid=(n // tc,),
        in_specs=[pl.BlockSpec((tc, width), lambda i: (i, 0)),
                  pl.BlockSpec((tc, width), lambda i: (i, 1)),
                  pl.BlockSpec((tc, width), lambda i: (i, 2)),
                  pl.BlockSpec((seqs, width), state_map),
                  pl.BlockSpec((seqs, width), state_map),
                  pl.BlockSpec((None, CONV_K, width), lambda i: (layer, 0, 0)),
                  pl.BlockSpec((None, 1, width), lambda i: (layer, 0, 0))],
        out_specs=[pl.BlockSpec((tc, width), lambda i: (i, 0)),
                   pl.BlockSpec((batch * 8, width), lambda i: (0, 0)),
                   pl.BlockSpec((seqs * 8, width), state_map)],
        out_shape=[jax.ShapeDtypeStruct((n, width), _BF16),
                   jax.ShapeDtypeStruct((batch * 8, width), _F32),
                   jax.ShapeDtypeStruct((n_streams * 8, width), _F32)],
        scratch_shapes=[pltpu.VMEM((8 + tc, width), _F32)],
        compiler_params=_params(("arbitrary",)),
        name="gated_conv",
    )(p, p, p, s0, s1, conv_w, g_conv)


def _outproj_kernel(ap_ref, as_ref, c_ref, w_ref, h_ref, gpost_ref, gnext_ref, ho_ref, xo_ref, a_buf,
                    *, width, n_prompt, n_tiles):
    i = pl.program_id(0)
    tm = a_buf.shape[0]
    first_mixed = n_prompt // tm

    @pl.when(i < first_mixed)
    def _():
        a_buf[...] = ap_ref[...]

    for t in range(first_mixed, n_tiles):
        p_rows = max(0, n_prompt - t * tm)
        s0 = t * tm + p_rows - n_prompt

        @pl.when(i == t)
        def _(p_rows=p_rows, s0=s0):
            if p_rows:
                a_buf[0:p_rows, :] = ap_ref[0:p_rows, :]
            a_buf[p_rows:tm, :] = as_ref[s0:s0 + tm - p_rows, :]

    y = (jnp.dot(a_buf[...], w_ref[0:width, :], preferred_element_type=_F32)
         + jnp.dot(c_ref[...], w_ref[width:2 * width, :], preferred_element_type=_F32))
    h_new = h_ref[...] + _rms(y, gpost_ref[...])
    ho_ref[...] = h_new
    xo_ref[...] = _rms(h_new, gnext_ref[...]).astype(xo_ref.dtype)


def _outproj(attn_p, attn_s, conv_n, w_out, h, g_post, g_next, layer, tm):
    n, d = h.shape
    n_prompt, width = attn_p.shape
    n_tiles = n // tm
    last_prompt_tile = (n_prompt - 1) // tm
    assert n_prompt % 16 == 0 and tm % 16 == 0
    gain = pl.BlockSpec((None, 1, d), lambda i: (layer, 0, 0))
    return pl.pallas_call(
        functools.partial(_outproj_kernel, width=width, n_prompt=n_prompt, n_tiles=n_tiles),
        grid=(n_tiles,),
        in_specs=[pl.BlockSpec((tm, width), lambda i: (jnp.minimum(i, last_prompt_tile), 0)),
                  pl.BlockSpec(attn_s.shape, lambda i: (0, 0)),
                  pl.BlockSpec((tm, width), lambda i: (i, 0)),
                  pl.BlockSpec((2 * width, d), lambda i: (0, 0)),
                  pl.BlockSpec((tm, d), lambda i: (i, 0)),
                  gain, gain],
        out_specs=[pl.BlockSpec((tm, d), lambda i: (i, 0)),
                   pl.BlockSpec((tm, d), lambda i: (i, 0))],
        out_shape=[jax.ShapeDtypeStruct((n, d), _F32),
                   jax.ShapeDtypeStruct((n, d), _BF16)],
        scratch_shapes=[pltpu.VMEM((tm, width), _BF16)],
        compiler_params=_params(("arbitrary",)),
        name="out_proj",
    )(attn_p, attn_s, conv_n, w_out, h, g_post, g_next)


def _ffn_kernel(x_ref, wg_ref, wu_ref, wd_ref, h_ref, gpost_ref, gnext_ref, ho_ref, xo_ref):
    c = pl.program_id(1)

    @pl.when(c == 0)
    def _():
        ho_ref[...] = jnp.zeros_like(ho_ref)

    tm = x_ref.shape[0]
    split = (tm // 2 + 15) // 16 * 16
    groups = [slice(0, split), slice(split, tm)]
    wg, wu, wd = wg_ref[...], wu_ref[...], wd_ref[...]
    gate_up = [(jnp.dot(x_ref[rs, :], wg, preferred_element_type=_F32),
                jnp.dot(x_ref[rs, :], wu, preferred_element_type=_F32)) for rs in groups]
    for rs, (gate, up) in zip(groups, gate_up):
        act = (gate * jax.nn.sigmoid(gate) * up).astype(_BF16)
        ho_ref[rs, :] += jnp.dot(act, wd, preferred_element_type=_F32)

    @pl.when(c == pl.num_programs(1) - 1)
    def _():
        h_new = h_ref[...] + _rms(ho_ref[...], gpost_ref[...])
        ho_ref[...] = h_new
        xo_ref[...] = _rms(h_new, gnext_ref[...]).astype(xo_ref.dtype)


def _ffn(xf, w_gate, w_up, w_down, h, g_post, g_next, layer, next_layer, tm):
    n, d = h.shape
    d_ff = w_gate.shape[1]
    tf = FF_TILE
    return pl.pallas_call(
        _ffn_kernel,
        grid=(n // tm, d_ff // tf),
        in_specs=[pl.BlockSpec((tm, d), lambda i, c: (i, 0)),
                  pl.BlockSpec((d, tf), lambda i, c: (0, c)),
                  pl.BlockSpec((d, tf), lambda i, c: (0, c)),
                  pl.BlockSpec((tf, d), lambda i, c: (c, 0)),
                  pl.BlockSpec((tm, d), lambda i, c: (i, 0)),
                  pl.BlockSpec((None, 1, d), lambda i, c: (layer, 0, 0)),
                  pl.BlockSpec((None, 1, d), lambda i, c: (next_layer, 0, 0))],
        out_specs=[pl.BlockSpec((tm, d), lambda i, c: (i, 0)),
                   pl.BlockSpec((tm, d), lambda i, c: (i, 0))],
        out_shape=[jax.ShapeDtypeStruct((n, d), _F32),
                   jax.ShapeDtypeStruct((n, d), _BF16)],
        compiler_params=_params(("arbitrary", "arbitrary")),
        name="swiglu_ffn",
    )(xf, w_gate, w_up, w_down, h, g_post, g_next)


def kernel(x_prompt, x_sample, cache_k, cache_v, state_conv, meta, w_in, w_out, conv_w, g_pre_mix, g_post_mix,
           g_attn_out, g_conv_out, g_pre_ffn, g_post_ffn, w_gate, w_up, w_down):
    batch, seq, d = x_prompt.shape
    n_streams, ds, _ = x_sample.shape
    depth = w_in.shape[0]
    width = w_out.shape[1] // 2
    n_heads = width // HEAD_DIM
    n_meta = meta.shape[0]
    past = cache_k.shape[2]
    t_real = n_meta + seq
    tp = -(-t_real // ATTN_TILE) * ATTN_TILE
    n_prompt = batch * tp
    n_sample = n_streams * ds
    n_tok = n_prompt + n_sample
    assert n_prompt % CONV_TILE == 0 and n_sample % CONV_TILE == 0 and CONV_TILE % ds == 0
    assert tp % ds == 0 and past % CACHE_TILE == 0 and ds >= CONV_K - 1
    assert w_gate.shape[2] % FF_TILE == 0 and cache_k.shape[3:] == (n_heads, HEAD_DIM)
    tm = _pick_tile(n_tok, ROW_TILE_TARGET)
    tm_in = _pick_tile(n_tok, IN_ROW_TILE_TARGET)
    scale = HEAD_DIM ** -0.5

    pad = jnp.zeros((tp - t_real, d), _F32)
    pieces = [piece for b in range(batch) for piece in (meta.astype(_F32), x_prompt[b], pad)]
    h = jnp.concatenate(pieces + [x_sample.reshape(n_sample, d)], axis=0)

    rows = lambda a: a.reshape(depth, 1, -1)
    g_pre_mix, g_post_mix, g_pre_ffn, g_post_ffn = rows(g_pre_mix), rows(g_post_mix), rows(g_pre_ffn), rows(g_post_ffn)
    g_attn_out, g_conv_out = rows(g_attn_out), rows(g_conv_out)
    cache_k2 = cache_k.reshape(-1, HEAD_DIM)
    cache_v2 = cache_v.reshape(-1, HEAD_DIM)

    xn = _norm(h, g_pre_mix, 0, tm)
    w_in_b = w_in[0].astype(_BF16)
    k_l, v_l, cp_l, cs_l = [], [], [], []
    for l in range(depth):
        p, pb, k3, v3 = _inproj(xn, w_in_b, tm_in, width)
        cast = [(w_out, l), (w_gate, l), (w_up, l), (w_down, l)] + ([(w_in, l + 1)] if l + 1 < depth else [])
        attn_p, casted = _attn_prompt(pb, g_attn_out, l, batch, tp, width, scale, cast)
        w_out_b, w_gate_b, w_up_b, w_down_b = casted[:4]
        w_in_b = casted[4] if l + 1 < depth else None
        attn_s = _attn_sample(pb, cache_k2, cache_v2, g_attn_out, l, n_prompt, n_streams, ds, past, width, scale)
        conv_n, tail_p, tail_s = _conv(p, state_conv[l, :, 0, :], state_conv[l, :, 1, :], conv_w, g_conv_out, l,
                                       tp, t_real, n_prompt, ds, width)
        h, xf = _outproj(attn_p, attn_s, conv_n, w_out_b, h, g_post_mix, g_pre_ffn, l, tm)
        h, xn = _ffn(xf, w_gate_b, w_up_b, w_down_b, h, g_post_ffn, g_pre_mix, l, min(l + 1, depth - 1), tm)
        k_l.append(k3.reshape(n_tok, n_heads, HEAD_DIM))
        v_l.append(v3.reshape(n_tok, n_heads, HEAD_DIM))
        cp_l.append(tail_p.reshape(batch, 8, width)[:, 8 - (CONV_K - 1):])
        cs_l.append(tail_s.reshape(n_streams, 8, width)[:, 8 - (CONV_K - 1):])

    def prompt_part(xs, lo, hi):
        rows_ = jnp.concatenate([x[b * tp + lo:b * tp + hi] for x in xs for b in range(batch)], axis=0)
        return rows_.reshape((len(xs), batch, hi - lo) + rows_.shape[1:])

    def sample_part(xs, lo):
        return jnp.stack([x[n_prompt:].reshape((n_streams, ds) + x.shape[1:])[:, lo:] for x in xs])

    y_prompt = prompt_part([h], n_meta, t_real)[0]
    y_sample = h[n_prompt:].reshape(n_streams, ds, d)
    return (y_prompt, y_sample,
            prompt_part(k_l, 0, t_real), prompt_part(v_l, 0, t_real), jnp.stack(cp_l),
            sample_part(k_l, 0), sample_part(v_l, 0), jnp.stack(cs_l))
```

```python
import functools

import jax
import jax.numpy as jnp
from jax import lax
from jax.experimental import pallas as pl
from jax.experimental.pallas import tpu as pltpu

HEAD_DIM = 128
CONV_K = 3
EPS = 1e-6
ATTN_TILE = 128
CACHE_TILE = 256
CONV_TILE = 256
ROW_TILE_TARGET = 600
IN_ROW_TILE_TARGET = 1200
IN_COL_TILE = 512
FF_TILE = 512
VMEM_LIMIT = 56 * 1024 * 1024
LOG2_E = 1.4426950408889634
LOG2_WEIGHT_FLOOR = -152.0

_F32 = jnp.float32
_BF16 = jnp.bfloat16
_NT = (((1,), (1,)), ((), ()))


def _pick_tile(n, target, mult=16):
    best = None
    for t in range(mult, target + 1, mult):
        if n % t == 0:
            best = t
    assert best is not None, (n, target)
    return best


def _params(sem):
    return pltpu.CompilerParams(dimension_semantics=sem, vmem_limit_bytes=VMEM_LIMIT)


def _rms(x, g):
    ms = jnp.mean(x * x, axis=-1, keepdims=True)
    return x * lax.rsqrt(ms + EPS) * g


def _head_slices(n_heads):
    return [slice(h * HEAD_DIM, (h + 1) * HEAD_DIM) for h in range(n_heads)]


def _norm_kernel(h_ref, g_ref, o_ref):
    o_ref[...] = _rms(h_ref[...], g_ref[...]).astype(o_ref.dtype)


def _norm(h, g, layer, tm):
    n, d = h.shape
    return pl.pallas_call(
        _norm_kernel,
        grid=(n // tm,),
        in_specs=[pl.BlockSpec((tm, d), lambda i: (i, 0)),
                  pl.BlockSpec((None, 1, d), lambda i: (layer, 0, 0))],
        out_specs=pl.BlockSpec((tm, d), lambda i: (i, 0)),
        out_shape=jax.ShapeDtypeStruct((n, d), _BF16),
        compiler_params=_params(("arbitrary",)),
        name="pre_norm",
    )(h, g)


def _inproj_kernel(x_ref, w_ref, p_ref, pb_ref, k_ref, v_ref, *, n_heads, tm, wpk):
    j = pl.program_id(1)
    acc = jnp.dot(x_ref[...], w_ref[...], preferred_element_type=_F32)
    p_ref[...] = acc
    pb_ref[...] = acc.astype(_BF16)

    heads_per_tile = n_heads // wpk
    for ref, first in ((k_ref, wpk), (v_ref, 2 * wpk)):
        for part in range(wpk):
            @pl.when(j == first + part)
            def _(ref=ref, part=part):
                for hh, hs in enumerate(_head_slices(heads_per_tile)):
                    ref[pl.ds(part * heads_per_tile + hh, tm, stride=n_heads), :] = p_ref[:, hs]


def _inproj(xn, w, tm, width):
    n, d = xn.shape
    n_heads = width // HEAD_DIM
    tn = min(IN_COL_TILE, width)
    wpk = width // tn
    kv_shape = jax.ShapeDtypeStruct((n * n_heads, HEAD_DIM), _F32)
    kv_spec = pl.BlockSpec((tm * n_heads, HEAD_DIM), lambda i, j: (i, 0))
    return pl.pallas_call(
        functools.partial(_inproj_kernel, n_heads=n_heads, tm=tm, wpk=wpk),
        grid=(n // tm, 6 * wpk),
        in_specs=[pl.BlockSpec((tm, d), lambda i, j: (i, 0)),
                  pl.BlockSpec((d, tn), lambda i, j: (0, j))],
        out_specs=[pl.BlockSpec((tm, tn), lambda i, j: (i, jnp.where(j < 3 * wpk, 3 * wpk, j - 3 * wpk))),
                   pl.BlockSpec((tm, tn), lambda i, j: (i, jnp.minimum(j, 3 * wpk))),
                   kv_spec, kv_spec],
        out_shape=[jax.ShapeDtypeStruct((n, 3 * width + tn), _F32),
                   jax.ShapeDtypeStruct((n, 3 * width + tn), _BF16),
                   kv_shape, kv_shape],
        compiler_params=_params(("arbitrary", "arbitrary")),
        name="in_proj",
    )(xn, w)


def _log2_sigmoids(z2):
    lsp = jnp.minimum(z2, 0.0) - jnp.log2(1.0 + jnp.exp2(-jnp.abs(z2)))
    return lsp, lsp - z2


def _suffix_sums(l_mats, uo):
    tq, tk = l_mats[0].shape
    sub, wc = uo.shape[0] // 2, uo.shape[1] // 2
    his = [l.astype(_BF16) for l in l_mats]
    los = [(l - hi.astype(_F32)).astype(_BF16) for l, hi in zip(l_mats, his)]
    if sub % HEAD_DIM:
        assert tk == sub
        top = uo[:tk]
        rs = [jnp.dot(hi, top, preferred_element_type=_F32) + jnp.dot(lo, top, preferred_element_type=_F32)
              for hi, lo in zip(his, los)]
        return [(r[:, :tk], r[:, wc:]) for r in rs]
    n_sub = tk // sub
    cols = [slice(b * sub, (b + 1) * sub) for b in range(n_sub)]
    lhs = jnp.concatenate([jnp.concatenate([hi[:, cs], lo[:, cs]], axis=1)
                           for hi, lo in zip(his, los) for cs in cols], axis=0)
    r = jnp.dot(lhs, uo, preferred_element_type=_F32)
    out = []
    for h in range(len(l_mats)):
        newer, afters = None, [None] * n_sub
        for b in reversed(range(n_sub)):
            r_hb = r[(h * n_sub + b) * tq:(h * n_sub + b + 1) * tq]
            afters[b] = r_hb[:, :sub] if newer is None else r_hb[:, :sub] + newer[:, :sub]
            newer = r_hb[:, wc:] if newer is None else newer + r_hb[:, wc:]
        out.append((afters[0] if n_sub == 1 else jnp.concatenate(afters, axis=1), newer))
    return out


def _suffix_matrix(rows, width):
    j = lax.rem(lax.broadcasted_iota(jnp.int32, (2 * rows, 2 * width), 0), rows)
    s = lax.broadcasted_iota(jnp.int32, (2 * rows, 2 * width), 1)
    return jnp.where((s >= width) | (j > s), 1.0, 0.0).astype(_BF16)


def _sb_tile(qs, ks, vs, cars, uo, vis, scale):
    tk = ks[0].shape[0]
    zs = [lax.dot_general(q, k, _NT, preferred_element_type=_F32) * (scale * LOG2_E) for q, k in zip(qs, ks)]
    lsps, lsns = [], []
    for z in zs:
        lsp, lsn = _log2_sigmoids(z)
        if vis is not None:
            lsn = jnp.where(vis, lsn, 0.0)
        lsps.append(lsp)
        lsns.append(lsn)
    sums = _suffix_sums(lsns, uo)
    pvs, new_cars = [], []
    for lsp, (after, total), v, car in zip(lsps, sums, vs, cars):
        x = lsp + after
        if car is not None:
            cw = car.shape[1]
            x = x + (car[:, :tk] if cw >= tk else jnp.concatenate([car] * (tk // cw), axis=1))
        w = jnp.exp2(x)
        if vis is not None:
            w = jnp.where(vis, w, 0.0)
        pvs.append(jnp.dot(w.astype(_BF16), v, preferred_element_type=_F32))
        new_cars.append(total if car is None else car + total[:, :cw])
    return pvs, new_cars


def _head_norm_store(o_ref, acc_ref, g_ref, n_heads):
    for hs in _head_slices(n_heads):
        o_ref[:, hs] = _rms(acc_ref[:, hs], g_ref[:, hs]).astype(o_ref.dtype)


def _attn_prompt_kernel(q_ref, k_ref, v_ref, uo_ref, g_ref, *rest, n_heads, scale, cast_every):
    n_cast = len(cast_every)
    cast_in, o_ref, cast_out = rest[:n_cast], rest[n_cast], rest[n_cast + 1:2 * n_cast + 1]
    acc_ref, car_ref = rest[2 * n_cast + 1:]
    tq = ATTN_TILE
    i = pl.program_id(1)
    heads = _head_slices(n_heads)

    _run_casts(pl.program_id(0) * pl.num_programs(1) + i, cast_in, cast_out, cast_every)

    def tile(blk, n_blk, vis, first):
        k0 = pl.multiple_of(blk * tq, tq)
        qs = [q_ref[:, hs] for hs in heads]
        ks = [k_ref[pl.ds(k0, n_blk * tq), hs] for hs in heads]
        vs = [v_ref[pl.ds(k0, n_blk * tq), hs] for hs in heads]
        cars = [None if first else car_ref[:, hs] for hs in heads]
        pvs, new_cars = _sb_tile(qs, ks, vs, cars, uo_ref[...], vis, scale)
        for hs, pv, car in zip(heads, pvs, new_cars):
            if first:
                acc_ref[:, hs] = pv
            else:
                acc_ref[:, hs] += pv
            car_ref[:, hs] = car

    def causal_last(n_blk):
        off = (n_blk - 1) * tq
        return (lax.broadcasted_iota(jnp.int32, (tq, n_blk * tq), 1) - off
                < lax.broadcasted_iota(jnp.int32, (tq, n_blk * tq), 0))

    @pl.when(i >= 2)
    def _():
        tile(i - 2, 3, causal_last(3), True)

    @pl.when(i < 2)
    def _():
        tile(i, 1, causal_last(1), True)

    def all_rows_done():
        return (jnp.max(car_ref[...]) < LOG2_WEIGHT_FLOOR).astype(jnp.int32)

    def cond(state):
        left, done = state
        return jnp.logical_and(left >= 2, done == 0)

    def body(state):
        left, _ = state
        tile(left - 2, 2, None, False)
        return left - 2, all_rows_done()

    left, done = lax.while_loop(cond, body, (jnp.where(i >= 2, i - 2, i), all_rows_done()))

    @pl.when(jnp.logical_and(left == 1, done == 0))
    def _():
        tile(0, 1, None, False)

    _head_norm_store(o_ref, acc_ref, g_ref, n_heads)


def _cast_plan(rows, n_steps):
    n_blocks = 1
    while n_blocks * 2 <= n_steps and rows % (n_blocks * 2) == 0 and (rows // (n_blocks * 2)) % 16 == 0:
        n_blocks *= 2
    every = 1
    while every * 2 * n_blocks <= n_steps:
        every *= 2
    return every, n_blocks


def _cast_side_job(cast, n_steps, step_of):
    plans = tuple(_cast_plan(w.shape[1], n_steps) for w, _ in cast)

    def specs(w, lyr, plan):
        every, n_blocks = plan
        rows, cols = w.shape[1:]
        blk = rows // n_blocks
        pick = lambda *idx: jnp.minimum(step_of(*idx) // every, n_blocks - 1)
        return (pl.BlockSpec((None, blk, cols), lambda *idx: (lyr, pick(*idx), 0)),
                pl.BlockSpec((blk, cols), lambda *idx: (pick(*idx), 0)),
                jax.ShapeDtypeStruct((rows, cols), _BF16))

    all_specs = [specs(w, lyr, plan) for (w, lyr), plan in zip(cast, plans)]
    return plans, [s[0] for s in all_specs], [s[1] for s in all_specs], [s[2] for s in all_specs]


def _run_casts(step, cast_in, cast_out, plans):
    for src, dst, (every, n_blocks) in zip(cast_in, cast_out, plans):
        @pl.when(jnp.logical_and(step < every * n_blocks, lax.rem(step, every) == 0))
        def _(src=src, dst=dst):
            dst[...] = src[...].astype(_BF16)


def _attn_prompt(pb, g_attn, layer, batch, tp, width, scale, cast):
    tq = ATTN_TILE
    nq = tp // tq
    n_heads = width // HEAD_DIM
    uo = _suffix_matrix(tq, tq)
    plans, cast_in_specs, cast_out_specs, cast_shapes = _cast_side_job(cast, batch * nq, lambda b, i: b * nq + i)
    outs = pl.pallas_call(
        functools.partial(_attn_prompt_kernel, n_heads=n_heads, scale=scale, cast_every=plans),
        grid=(batch, nq),
        in_specs=[pl.BlockSpec((tq, width), lambda b, i: (b * nq + i, 0)),
                  pl.BlockSpec((tp, width), lambda b, i: (b, 1)),
                  pl.BlockSpec((tp, width), lambda b, i: (b, 2)),
                  pl.BlockSpec((2 * tq, 2 * tq), lambda b, i: (0, 0)),
                  pl.BlockSpec((None, 1, width), lambda b, i: (layer, 0, 0))] + cast_in_specs,
        out_specs=[pl.BlockSpec((tq, width), lambda b, i: (b * nq + i, 0))] + cast_out_specs,
        out_shape=[jax.ShapeDtypeStruct((batch * tp, width), _BF16)] + cast_shapes,
        scratch_shapes=[pltpu.VMEM((tq, width), _F32), pltpu.VMEM((tq, width), _F32)],
        compiler_params=_params(("arbitrary", "arbitrary")),
        name="attn_prompt",
    )(pb, pb, pb, uo, g_attn, *[w for w, _ in cast])
    return outs[0], outs[1:]


def _attn_sample_kernel(q_ref, kn_ref, vn_ref, ck_hbm, cv_hbm, uos_ref, uoc_ref, g_ref, *rest,
                        n_heads, scale, ds, past, stream0, cast_every):
    n_cast = len(cast_every)
    cast_in, o_ref, cast_out = rest[:n_cast], rest[n_cast], rest[n_cast + 1:2 * n_cast + 1]
    acc_ref, car_ref, kbuf, vbuf, sem = rest[2 * n_cast + 1:]
    tk = CACHE_TILE
    n_ct = past // tk
    rows = tk * n_heads
    b = pl.program_id(0)
    heads = _head_slices(n_heads)
    _run_casts(b, cast_in, cast_out, cast_every)

    def copies(stream, blk, slot):
        row0 = pl.multiple_of(((stream0 + stream) * past + blk * tk) * n_heads, rows)
        return (pltpu.make_async_copy(ck_hbm.at[pl.ds(row0, rows), :], kbuf.at[slot], sem.at[0, slot]),
                pltpu.make_async_copy(cv_hbm.at[pl.ds(row0, rows), :], vbuf.at[slot], sem.at[1, slot]))

    def start(stream, blk, slot):
        for c in copies(stream, blk, slot):
            c.start()

    def wait(stream, blk, slot):
        for c in copies(stream, blk, slot):
            c.wait()

    newest = n_ct - 1
    par = lax.rem(b, 2)

    @pl.when(b == 0)
    def _():
        start(b, newest, par)

    @pl.when(b + 1 < pl.num_programs(0))
    def _():
        start(b + 1, newest, 1 - par)

    causal = (lax.broadcasted_iota(jnp.int32, (ds, ds), 1)
              < lax.broadcasted_iota(jnp.int32, (ds, ds), 0))
    qs = [q_ref[:, hs] for hs in heads]
    pvs, cars = _sb_tile(qs, [kn_ref[:, hs] for hs in heads], [vn_ref[:, hs] for hs in heads],
                         [None] * n_heads, uos_ref[...], causal, scale)
    for h, hs in enumerate(heads):
        acc_ref[:, hs] = pvs[h]
        car_ref[h] = cars[h]

    def cache_tile(slot):
        kslot, vslot = kbuf.at[slot], vbuf.at[slot]
        ks = [kslot[pl.ds(h, tk, stride=n_heads), :].astype(_BF16) for h in range(n_heads)]
        vs = [vslot[pl.ds(h, tk, stride=n_heads), :].astype(_BF16) for h in range(n_heads)]
        pvs, cars = _sb_tile([q_ref[:, hs] for hs in heads], ks, vs,
                             [car_ref[h] for h in range(n_heads)], uoc_ref[...], None, scale)
        for h, hs in enumerate(heads):
            acc_ref[:, hs] += pvs[h]
            car_ref[h] = cars[h]
        return (jnp.max(car_ref[...]) < LOG2_WEIGHT_FLOOR).astype(jnp.int32)

    wait(b, newest, par)
    done = cache_tile(par)

    if n_ct > 1:
        older = lambda blk: 2 + lax.rem(blk, 2)

        @pl.when(done == 0)
        def _():
            start(b, newest - 1, older(newest - 1))

        def cond(state):
            blk, done = state
            return jnp.logical_and(blk >= 0, done == 0)

        def body(state):
            blk, _ = state
            wait(b, blk, older(blk))

            @pl.when(blk > 0)
            def _():
                start(b, blk - 1, older(blk - 1))

            return blk - 1, cache_tile(older(blk))

        blk_end, _ = lax.while_loop(cond, body, (jnp.int32(newest - 1), done))

        @pl.when(jnp.logical_and(blk_end >= 0, blk_end < newest - 1))
        def _():
            wait(b, blk_end, older(blk_end))

    _head_norm_store(o_ref, acc_ref, g_ref, n_heads)


def _attn_sample(pb, cache_k, cache_v, g_attn, layer, row0, n_streams, ds, past, width, scale, cast):
    tk = CACHE_TILE
    n_heads = width // HEAD_DIM
    blk0 = row0 // ds
    uos = _suffix_matrix(ds, tk)
    uoc = _suffix_matrix(tk, tk)
    plans, cast_in_specs, cast_out_specs, cast_shapes = _cast_side_job(cast, n_streams, lambda b: b)
    outs = pl.pallas_call(
        functools.partial(_attn_sample_kernel, n_heads=n_heads, scale=scale, ds=ds, past=past,
                          stream0=layer * n_streams, cast_every=plans),
        grid=(n_streams,),
        in_specs=[pl.BlockSpec((ds, width), lambda b: (blk0 + b, 0)),
                  pl.BlockSpec((ds, width), lambda b: (blk0 + b, 1)),
                  pl.BlockSpec((ds, width), lambda b: (blk0 + b, 2)),
                  pl.BlockSpec(memory_space=pl.ANY),
                  pl.BlockSpec(memory_space=pl.ANY),
                  pl.BlockSpec((2 * ds, 2 * tk), lambda b: (0, 0)),
                  pl.BlockSpec((2 * tk, 2 * tk), lambda b: (0, 0)),
                  pl.BlockSpec((None, 1, width), lambda b: (layer, 0, 0))] + cast_in_specs,
        out_specs=[pl.BlockSpec((ds, width), lambda b: (b, 0))] + cast_out_specs,
        out_shape=[jax.ShapeDtypeStruct((n_streams * ds, width), _BF16)] + cast_shapes,
        scratch_shapes=[pltpu.VMEM((ds, width), _F32),
                        pltpu.VMEM((n_heads, ds, tk), _F32),
                        pltpu.VMEM((4, tk * n_heads, HEAD_DIM), _F32),
                        pltpu.VMEM((4, tk * n_heads, HEAD_DIM), _F32),
                        pltpu.SemaphoreType.DMA((2, 4))],
        compiler_params=_params(("arbitrary",)),
        name="attn_sample",
    )(pb, pb, pb, cache_k, cache_v, uos, uoc, g_attn, *[w for w, _ in cast])
    return outs[0], outs[1:]


def _conv_kernel(gb_ref, gc_ref, ci_ref, s0_ref, s1_ref, w_ref, g_ref, o_ref, tail_p_ref, tail_s_ref, ubuf_ref,
                 *, tp, t_real, n_prompt, ds, width):
    tc = CONV_TILE
    i = pl.program_id(0)

    @pl.when(i == 0)
    def _():
        ubuf_ref[0:8, :] = jnp.zeros((8, width), _F32)

    u = gc_ref[...] * ci_ref[...]
    ubuf_ref[8:8 + tc, :] = u

    for b in range(n_prompt // tp):
        g0 = b * tp + t_real - 8

        @pl.when(i == g0 // tc)
        def _(b=b, g0=g0):
            tail_p_ref[8 * b:8 * b + 8, :] = u[g0 % tc:g0 % tc + 8, :]

    @pl.when(i >= n_prompt // tc)
    def _():
        for s in range(tc // ds):
            tail_s_ref[8 * s:8 * s + 8, :] = u[ds * s + ds - 8:ds * s + ds, :]

    u1 = ubuf_ref[7:7 + tc, :]
    u2 = ubuf_ref[6:6 + tc, :]

    row = i * tc + lax.broadcasted_iota(jnp.int32, (tc, 1), 0)
    is_sample = row >= n_prompt
    pos = jnp.where(is_sample, lax.rem(row - n_prompt, ds), lax.rem(row, tp))
    first = pos == 0
    second = pos == 1
    seqs = tc // ds
    zero = jnp.zeros((tc, width), _F32)
    st0 = jnp.broadcast_to(s0_ref[...][:, None, :], (seqs, ds, width)).reshape(tc, width)
    st1 = jnp.broadcast_to(s1_ref[...][:, None, :], (seqs, ds, width)).reshape(tc, width)
    st0 = jnp.where(is_sample, st0, zero)
    st1 = jnp.where(is_sample, st1, zero)
    u1 = jnp.where(first, st1, u1)
    u2 = jnp.where(first, st0, jnp.where(second, st1, u2))
    y = u2 * w_ref[0:1, :] + u1 * w_ref[1:2, :] + u * w_ref[2:3, :]
    c = gb_ref[...] * y
    for gs in _head_slices(width // HEAD_DIM):
        o_ref[:, gs] = _rms(c[:, gs], g_ref[:, gs]).astype(o_ref.dtype)
    ubuf_ref[0:8, :] = u[tc - 8:tc, :]


def _conv(p, s0, s1, conv_w, g_conv, layer, tp, t_real, n_prompt, ds, width):
    tc = CONV_TILE
    n = p.shape[0]
    seqs = tc // ds
    first_sample_tile = n_prompt // tc
    batch = n_prompt // tp
    n_streams = (n - n_prompt) // ds
    assert t_real % 8 == 0 and ds % 8 == 0

    def state_map(i):
        return (jnp.maximum(i - first_sample_tile, 0), 0)

    return pl.pallas_call(
        functools.partial(_conv_kernel, tp=tp, t_real=t_real, n_prompt=n_prompt, ds=ds, width=width),
        grid=(n // tc,),
        in_specs=[pl.BlockSpec((tc, width), lambda i: (i, 0)),
                  pl.BlockSpec((tc, width), lambda i: (i, 1)),
                  pl.BlockSpec((tc, width), lambda i: (i, 2)),
                  pl.BlockSpec((seqs, width), state_map),
                  pl.BlockSpec((seqs, width), state_map),
                  pl.BlockSpec((None, CONV_K, width), lambda i: (layer, 0, 0)),
                  pl.BlockSpec((None, 1, width), lambda i: (layer, 0, 0))],
        out_specs=[pl.BlockSpec((tc, width), lambda i: (i, 0)),
                   pl.BlockSpec((batch * 8, width), lambda i: (0, 0)),
                   pl.BlockSpec((seqs * 8, width), state_map)],
        out_shape=[jax.ShapeDtypeStruct((n, width), _BF16),
                   jax.ShapeDtypeStruct((batch * 8, width), _F32),
                   jax.ShapeDtypeStruct((n_streams * 8, width), _F32)],
        scratch_shapes=[pltpu.VMEM((8 + tc, width), _F32)],
        compiler_params=_params(("arbitrary",)),
        name="gated_conv",
    )(p, p, p, s0, s1, conv_w, g_conv)


def _outproj_kernel(ap_ref, as_ref, c_ref, w_ref, h_ref, gpost_ref, gnext_ref, ho_ref, xo_ref, a_buf,
                    *, width, n_prompt, n_tiles):
    i = pl.program_id(0)
    tm = a_buf.shape[0]
    first_mixed = n_prompt // tm

    @pl.when(i < first_mixed)
    def _():
        a_buf[...] = ap_ref[...]

    for t in range(first_mixed, n_tiles):
        p_rows = max(0, n_prompt - t * tm)
        s0 = t * tm + p_rows - n_prompt

        @pl.when(i == t)
        def _(p_rows=p_rows, s0=s0):
            if p_rows:
                a_buf[0:p_rows, :] = ap_ref[0:p_rows, :]
            a_buf[p_rows:tm, :] = as_ref[s0:s0 + tm - p_rows, :]

    y = (jnp.dot(a_buf[...], w_ref[0:width, :], preferred_element_type=_F32)
         + jnp.dot(c_ref[...], w_ref[width:2 * width, :], preferred_element_type=_F32))
    h_new = h_ref[...] + _rms(y, gpost_ref[...])
    ho_ref[...] = h_new
    xo_ref[...] = _rms(h_new, gnext_ref[...]).astype(xo_ref.dtype)


def _outproj(attn_p, attn_s, conv_n, w_out, h, g_post, g_next, layer, tm):
    n, d = h.shape
    n_prompt, width = attn_p.shape
    n_tiles = n // tm
    last_prompt_tile = (n_prompt - 1) // tm
    assert n_prompt % 16 == 0 and tm % 16 == 0
    gain = pl.BlockSpec((None, 1, d), lambda i: (layer, 0, 0))
    return pl.pallas_call(
        functools.partial(_outproj_kernel, width=width, n_prompt=n_prompt, n_tiles=n_tiles),
        grid=(n_tiles,),
        in_specs=[pl.BlockSpec((tm, width), lambda i: (jnp.minimum(i, last_prompt_tile), 0)),
                  pl.BlockSpec(attn_s.shape, lambda i: (0, 0)),
                  pl.BlockSpec((tm, width), lambda i: (i, 0)),
                  pl.BlockSpec((2 * width, d), lambda i: (0, 0)),
                  pl.BlockSpec((tm, d), lambda i: (i, 0)),
                  gain, gain],
        out_specs=[pl.BlockSpec((tm, d), lambda i: (i, 0)),
                   pl.BlockSpec((tm, d), lambda i: (i, 0))],
        out_shape=[jax.ShapeDtypeStruct((n, d), _F32),
                   jax.ShapeDtypeStruct((n, d), _BF16)],
        scratch_shapes=[pltpu.VMEM((tm, width), _BF16)],
        compiler_params=_params(("arbitrary",)),
        name="out_proj",
    )(attn_p, attn_s, conv_n, w_out, h, g_post, g_next)


def _ffn_kernel(x_ref, wg_ref, wu_ref, wd_ref, h_ref, gpost_ref, gnext_ref, ho_ref, xo_ref):
    c = pl.program_id(1)

    @pl.when(c == 0)
    def _():
        ho_ref[...] = jnp.zeros_like(ho_ref)

    tm = x_ref.shape[0]
    split = (tm // 2 + 15) // 16 * 16
    groups = [slice(0, split), slice(split, tm)]
    wg, wu, wd = wg_ref[...], wu_ref[...], wd_ref[...]
    gate_up = [(jnp.dot(x_ref[rs, :], wg, preferred_element_type=_F32),
                jnp.dot(x_ref[rs, :], wu, preferred_element_type=_F32)) for rs in groups]
    for rs, (gate, up) in zip(groups, gate_up):
        act = (gate * jax.nn.sigmoid(gate) * up).astype(_BF16)
        ho_ref[rs, :] += jnp.dot(act, wd, preferred_element_type=_F32)

    @pl.when(c == pl.num_programs(1) - 1)
    def _():
        h_new = h_ref[...] + _rms(ho_ref[...], gpost_ref[...])
        ho_ref[...] = h_new
        xo_ref[...] = _rms(h_new, gnext_ref[...]).astype(xo_ref.dtype)


def _ffn(xf, w_gate, w_up, w_down, h, g_post, g_next, layer, next_layer, tm):
    n, d = h.shape
    d_ff = w_gate.shape[1]
    tf = FF_TILE
    return pl.pallas_call(
        _ffn_kernel,
        grid=(n // tm, d_ff // tf),
        in_specs=[pl.BlockSpec((tm, d), lambda i, c: (i, 0)),
                  pl.BlockSpec((d, tf), lambda i, c: (0, c)),
                  pl.BlockSpec((d, tf), lambda i, c: (0, c)),
                  pl.BlockSpec((tf, d), lambda i, c: (c, 0)),
                  pl.BlockSpec((tm, d), lambda i, c: (i, 0)),
                  pl.BlockSpec((None, 1, d), lambda i, c: (layer, 0, 0)),
                  pl.BlockSpec((None, 1, d), lambda i, c: (next_layer, 0, 0))],
        out_specs=[pl.BlockSpec((tm, d), lambda i, c: (i, 0)),
                   pl.BlockSpec((tm, d), lambda i, c: (i, 0))],
        out_shape=[jax.ShapeDtypeStruct((n, d), _F32),
                   jax.ShapeDtypeStruct((n, d), _BF16)],
        compiler_params=_params(("arbitrary", "arbitrary")),
        name="swiglu_ffn",
    )(xf, w_gate, w_up, w_down, h, g_post, g_next)


def kernel(x_prompt, x_sample, cache_k, cache_v, state_conv, meta, w_in, w_out, conv_w, g_pre_mix, g_post_mix,
           g_attn_out, g_conv_out, g_pre_ffn, g_post_ffn, w_gate, w_up, w_down):
    batch, seq, d = x_prompt.shape
    n_streams, ds, _ = x_sample.shape
    depth = w_in.shape[0]
    width = w_out.shape[1] // 2
    n_heads = width // HEAD_DIM
    n_meta = meta.shape[0]
    past = cache_k.shape[2]
    t_real = n_meta + seq
    tp = -(-t_real // ATTN_TILE) * ATTN_TILE
    n_prompt = batch * tp
    n_sample = n_streams * ds
    n_tok = n_prompt + n_sample
    assert n_prompt % CONV_TILE == 0 and n_sample % CONV_TILE == 0 and CONV_TILE % ds == 0
    assert tp % ds == 0 and past % CACHE_TILE == 0 and ds >= CONV_K - 1
    assert w_gate.shape[2] % FF_TILE == 0 and cache_k.shape[3:] == (n_heads, HEAD_DIM)
    tm = _pick_tile(n_tok, ROW_TILE_TARGET)
    tm_in = _pick_tile(n_tok, IN_ROW_TILE_TARGET)
    scale = HEAD_DIM ** -0.5

    pad = jnp.zeros((tp - t_real, d), _F32)
    pieces = [piece for b in range(batch) for piece in (meta.astype(_F32), x_prompt[b], pad)]
    h = jnp.concatenate(pieces + [x_sample.reshape(n_sample, d)], axis=0)

    rows = lambda a: a.reshape(depth, 1, -1)
    g_pre_mix, g_post_mix, g_pre_ffn, g_post_ffn = rows(g_pre_mix), rows(g_post_mix), rows(g_pre_ffn), rows(g_post_ffn)
    g_attn_out, g_conv_out = rows(g_attn_out), rows(g_conv_out)
    cache_k2 = cache_k.reshape(-1, HEAD_DIM)
    cache_v2 = cache_v.reshape(-1, HEAD_DIM)

    xn = _norm(h, g_pre_mix, 0, tm)
    w_in_b = w_in[0].astype(_BF16)
    k_l, v_l, cp_l, cs_l = [], [], [], []
    for l in range(depth):
        p, pb, k3, v3 = _inproj(xn, w_in_b, tm_in, width)
        attn_p, (w_gate_b, w_up_b, w_down_b) = _attn_prompt(
            pb, g_attn_out, l, batch, tp, width, scale, [(w_gate, l), (w_up, l), (w_down, l)])
        attn_s, casted = _attn_sample(
            pb, cache_k2, cache_v2, g_attn_out, l, n_prompt, n_streams, ds, past, width, scale,
            [(w_out, l)] + ([(w_in, l + 1)] if l + 1 < depth else []))
        w_out_b = casted[0]
        w_in_b = casted[1] if l + 1 < depth else None
        conv_n, tail_p, tail_s = _conv(p, state_conv[l, :, 0, :], state_conv[l, :, 1, :], conv_w, g_conv_out, l,
                                       tp, t_real, n_prompt, ds, width)
        h, xf = _outproj(attn_p, attn_s, conv_n, w_out_b, h, g_post_mix, g_pre_ffn, l, tm)
        h, xn = _ffn(xf, w_gate_b, w_up_b, w_down_b, h, g_post_ffn, g_pre_mix, l, min(l + 1, depth - 1), tm)
        k_l.append(k3.reshape(n_tok, n_heads, HEAD_DIM))
        v_l.append(v3.reshape(n_tok, n_heads, HEAD_DIM))
        cp_l.append(tail_p.reshape(batch, 8, width)[:, 8 - (CONV_K - 1):])
        cs_l.append(tail_s.reshape(n_streams, 8, width)[:, 8 - (CONV_K - 1):])

    def prompt_part(xs, lo, hi):
        rows_ = jnp.concatenate([x[b * tp + lo:b * tp + hi] for x in xs for b in range(batch)], axis=0)
        return rows_.reshape((len(xs), batch, hi - lo) + rows_.shape[1:])

    def sample_part(xs, lo):
        return jnp.stack([x[n_prompt:].reshape((n_streams, ds) + x.shape[1:])[:, lo:] for x in xs])

    y_prompt = prompt_part([h], n_meta, t_real)[0]
    y_sample = h[n_prompt:].reshape(n_streams, ds, d)
    return (y_prompt, y_sample,
            prompt_part(k_l, 0, t_real), prompt_part(v_l, 0, t_real), jnp.stack(cp_l),
            sample_part(k_l, 0), sample_part(v_l, 0), jnp.stack(cs_l))
```

```python
import functools

import jax
import jax.numpy as jnp
from jax import lax
from jax.experimental import pallas as pl
from jax.experimental.pallas import tpu as pltpu

HEAD_DIM = 128
CONV_K = 3
EPS = 1e-6
ATTN_TILE = 128
CACHE_TILE = 256
CONV_TILE = 256
ROW_TILE_TARGET = 600
IN_ROW_TILE_TARGET = 1200
IN_COL_TILE = 512
FF_TILE = 512
VMEM_LIMIT = 56 * 1024 * 1024
LOG2_E = 1.4426950408889634
LOG2_WEIGHT_FLOOR = -152.0

_F32 = jnp.float32
_BF16 = jnp.bfloat16
_NT = (((1,), (1,)), ((), ()))


def _pick_tile(n, target, mult=16):
    best = None
    for t in range(mult, target + 1, mult):
        if n % t == 0:
            best = t
    assert best is not None, (n, target)
    return best


def _params(sem):
    return pltpu.CompilerParams(dimension_semantics=sem, vmem_limit_bytes=VMEM_LIMIT)


def _rms(x, g):
    ms = jnp.mean(x * x, axis=-1, keepdims=True)
    return x * lax.rsqrt(ms + EPS) * g


def _head_slices(n_heads):
    return [slice(h * HEAD_DIM, (h + 1) * HEAD_DIM) for h in range(n_heads)]


def _norm_kernel(h_ref, g_ref, o_ref):
    o_ref[...] = _rms(h_ref[...], g_ref[...]).astype(o_ref.dtype)


def _norm(h, g, layer, tm):
    n, d = h.shape
    return pl.pallas_call(
        _norm_kernel,
        grid=(n // tm,),
        in_specs=[pl.BlockSpec((tm, d), lambda i: (i, 0)),
                  pl.BlockSpec((None, 1, d), lambda i: (layer, 0, 0))],
        out_specs=pl.BlockSpec((tm, d), lambda i: (i, 0)),
        out_shape=jax.ShapeDtypeStruct((n, d), _BF16),
        compiler_params=_params(("arbitrary",)),
        name="pre_norm",
    )(h, g)


def _inproj_kernel(x_ref, w_ref, p_ref, pb_ref, k_ref, v_ref, *, n_heads, tm, wpk):
    j = pl.program_id(1)
    acc = jnp.dot(x_ref[...], w_ref[...], preferred_element_type=_F32)
    p_ref[...] = acc
    pb_ref[...] = acc.astype(_BF16)

    heads_per_tile = n_heads // wpk
    for ref, first in ((k_ref, wpk), (v_ref, 2 * wpk)):
        for part in range(wpk):
            @pl.when(j == first + part)
            def _(ref=ref, part=part):
                for hh, hs in enumerate(_head_slices(heads_per_tile)):
                    ref[pl.ds(part * heads_per_tile + hh, tm, stride=n_heads), :] = p_ref[:, hs]


def _inproj(xn, w, tm, width):
    n, d = xn.shape
    n_heads = width // HEAD_DIM
    tn = min(IN_COL_TILE, width)
    wpk = width // tn
    kv_shape = jax.ShapeDtypeStruct((n * n_heads, HEAD_DIM), _F32)
    kv_spec = pl.BlockSpec((tm * n_heads, HEAD_DIM), lambda i, j: (i, 0))
    return pl.pallas_call(
        functools.partial(_inproj_kernel, n_heads=n_heads, tm=tm, wpk=wpk),
        grid=(n // tm, 6 * wpk),
        in_specs=[pl.BlockSpec((tm, d), lambda i, j: (i, 0)),
                  pl.BlockSpec((d, tn), lambda i, j: (0, j))],
        out_specs=[pl.BlockSpec((tm, tn), lambda i, j: (i, jnp.where(j < 3 * wpk, 3 * wpk, j - 3 * wpk))),
                   pl.BlockSpec((tm, tn), lambda i, j: (i, jnp.minimum(j, 3 * wpk))),
                   kv_spec, kv_spec],
        out_shape=[jax.ShapeDtypeStruct((n, 3 * width + tn), _F32),
                   jax.ShapeDtypeStruct((n, 3 * width + tn), _BF16),
                   kv_shape, kv_shape],
        compiler_params=_params(("arbitrary", "arbitrary")),
        name="in_proj",
    )(xn, w)


def _log2_sigmoids(z2):
    lsp = jnp.minimum(z2, 0.0) - jnp.log2(1.0 + jnp.exp2(-jnp.abs(z2)))
    return lsp, lsp - z2


def _suffix_sums(l_mats, uo):
    tq, tk = l_mats[0].shape
    sub, wc = uo.shape[0] // 2, uo.shape[1] // 2
    his = [l.astype(_BF16) for l in l_mats]
    los = [(l - hi.astype(_F32)).astype(_BF16) for l, hi in zip(l_mats, his)]
    if sub % HEAD_DIM:
        assert tk == sub
        top = uo[:tk]
        rs = [jnp.dot(hi, top, preferred_element_type=_F32) + jnp.dot(lo, top, preferred_element_type=_F32)
              for hi, lo in zip(his, los)]
        return [(r[:, :tk], r[:, wc:]) for r in rs]
    n_sub = tk // sub
    cols = [slice(b * sub, (b + 1) * sub) for b in range(n_sub)]
    lhs = jnp.concatenate([jnp.concatenate([hi[:, cs], lo[:, cs]], axis=1)
                           for hi, lo in zip(his, los) for cs in cols], axis=0)
    r = jnp.dot(lhs, uo, preferred_element_type=_F32)
    out = []
    for h in range(len(l_mats)):
        newer, afters = None, [None] * n_sub
        for b in reversed(range(n_sub)):
            r_hb = r[(h * n_sub + b) * tq:(h * n_sub + b + 1) * tq]
            afters[b] = r_hb[:, :sub] if newer is None else r_hb[:, :sub] + newer[:, :sub]
            newer = r_hb[:, wc:] if newer is None else newer + r_hb[:, wc:]
        out.append((afters[0] if n_sub == 1 else jnp.concatenate(afters, axis=1), newer))
    return out


def _suffix_matrix(rows, width):
    j = lax.rem(lax.broadcasted_iota(jnp.int32, (2 * rows, 2 * width), 0), rows)
    s = lax.broadcasted_iota(jnp.int32, (2 * rows, 2 * width), 1)
    return jnp.where((s >= width) | (j > s), 1.0, 0.0).astype(_BF16)


def _sb_tile(qs, ks, vs, cars, uo, vis, scale):
    tk = ks[0].shape[0]
    zs = [lax.dot_general(q, k, _NT, preferred_element_type=_F32) * (scale * LOG2_E) for q, k in zip(qs, ks)]
    lsps, lsns = [], []
    for z in zs:
        lsp, lsn = _log2_sigmoids(z)
        if vis is not None:
            lsn = jnp.where(vis, lsn, 0.0)
        lsps.append(lsp)
        lsns.append(lsn)
    sums = _suffix_sums(lsns, uo)
    pvs, new_cars = [], []
    for lsp, (after, total), v, car in zip(lsps, sums, vs, cars):
        x = lsp + after
        if car is not None:
            cw = car.shape[1]
            x = x + (car[:, :tk] if cw >= tk else jnp.concatenate([car] * (tk // cw), axis=1))
        w = jnp.exp2(x)
        if vis is not None:
            w = jnp.where(vis, w, 0.0)
        pvs.append(jnp.dot(w.astype(_BF16), v, preferred_element_type=_F32))
        new_cars.append(total if car is None else car + total[:, :cw])
    return pvs, new_cars


def _head_norm_store(o_ref, acc_ref, g_ref, n_heads):
    for hs in _head_slices(n_heads):
        o_ref[:, hs] = _rms(acc_ref[:, hs], g_ref[:, hs]).astype(o_ref.dtype)


def _attn_prompt_kernel(q_ref, k_ref, v_ref, uo_ref, g_ref, *rest, n_heads, scale, cast_every):
    n_cast = len(cast_every)
    cast_in, o_ref, cast_out = rest[:n_cast], rest[n_cast], rest[n_cast + 1:2 * n_cast + 1]
    acc_ref, car_ref = rest[2 * n_cast + 1:]
    tq = ATTN_TILE
    i = pl.program_id(1)
    heads = _head_slices(n_heads)

    _run_casts(pl.program_id(0) * pl.num_programs(1) + i, cast_in, cast_out, cast_every)

    def tile(blk, n_blk, vis, first):
        k0 = pl.multiple_of(blk * tq, tq)
        qs = [q_ref[:, hs] for hs in heads]
        ks = [k_ref[pl.ds(k0, n_blk * tq), hs] for hs in heads]
        vs = [v_ref[pl.ds(k0, n_blk * tq), hs] for hs in heads]
        cars = [None if first else car_ref[:, hs] for hs in heads]
        pvs, new_cars = _sb_tile(qs, ks, vs, cars, uo_ref[...], vis, scale)
        for hs, pv, car in zip(heads, pvs, new_cars):
            if first:
                acc_ref[:, hs] = pv
            else:
                acc_ref[:, hs] += pv
            car_ref[:, hs] = car

    def causal_last(n_blk):
        off = (n_blk - 1) * tq
        return (lax.broadcasted_iota(jnp.int32, (tq, n_blk * tq), 1) - off
                < lax.broadcasted_iota(jnp.int32, (tq, n_blk * tq), 0))

    @pl.when(i >= 2)
    def _():
        tile(i - 2, 3, causal_last(3), True)

    @pl.when(i < 2)
    def _():
        tile(i, 1, causal_last(1), True)

    def all_rows_done():
        return (jnp.max(car_ref[...]) < LOG2_WEIGHT_FLOOR).astype(jnp.int32)

    def cond(state):
        left, done = state
        return jnp.logical_and(left >= 2, done == 0)

    def body(state):
        left, _ = state
        tile(left - 2, 2, None, False)
        return left - 2, all_rows_done()

    left, done = lax.while_loop(cond, body, (jnp.where(i >= 2, i - 2, i), all_rows_done()))

    @pl.when(jnp.logical_and(left == 1, done == 0))
    def _():
        tile(0, 1, None, False)

    _head_norm_store(o_ref, acc_ref, g_ref, n_heads)


def _cast_plan(rows, n_steps):
    n_blocks = 1
    while n_blocks * 2 <= n_steps and rows % (n_blocks * 2) == 0 and (rows // (n_blocks * 2)) % 16 == 0:
        n_blocks *= 2
    every = 1
    while every * 2 * n_blocks <= n_steps:
        every *= 2
    return every, n_blocks


def _cast_side_job(cast, n_steps, step_of):
    plans = tuple(_cast_plan(w.shape[1], n_steps) for w, _ in cast)

    def specs(w, lyr, plan):
        every, n_blocks = plan
        rows, cols = w.shape[1:]
        blk = rows // n_blocks
        pick = lambda *idx: jnp.minimum(step_of(*idx) // every, n_blocks - 1)
        return (pl.BlockSpec((None, blk, cols), lambda *idx: (lyr, pick(*idx), 0)),
                pl.BlockSpec((blk, cols), lambda *idx: (pick(*idx), 0)),
                jax.ShapeDtypeStruct((rows, cols), _BF16))

    all_specs = [specs(w, lyr, plan) for (w, lyr), plan in zip(cast, plans)]
    return plans, [s[0] for s in all_specs], [s[1] for s in all_specs], [s[2] for s in all_specs]


def _run_casts(step, cast_in, cast_out, plans):
    for src, dst, (every, n_blocks) in zip(cast_in, cast_out, plans):
        @pl.when(jnp.logical_and(step < every * n_blocks, lax.rem(step, every) == 0))
        def _(src=src, dst=dst):
            dst[...] = src[...].astype(_BF16)


def _attn_prompt(pb, g_attn, layer, batch, tp, width, scale, cast):
    tq = ATTN_TILE
    nq = tp // tq
    n_heads = width // HEAD_DIM
    uo = _suffix_matrix(tq, tq)
    plans, cast_in_specs, cast_out_specs, cast_shapes = _cast_side_job(cast, batch * nq, lambda b, i: b * nq + i)
    outs = pl.pallas_call(
        functools.partial(_attn_prompt_kernel, n_heads=n_heads, scale=scale, cast_every=plans),
        grid=(batch, nq),
        in_specs=[pl.BlockSpec((tq, width), lambda b, i: (b * nq + i, 0)),
                  pl.BlockSpec((tp, width), lambda b, i: (b, 1)),
                  pl.BlockSpec((tp, width), lambda b, i: (b, 2)),
                  pl.BlockSpec((2 * tq, 2 * tq), lambda b, i: (0, 0)),
                  pl.BlockSpec((None, 1, width), lambda b, i: (layer, 0, 0))] + cast_in_specs,
        out_specs=[pl.BlockSpec((tq, width), lambda b, i: (b * nq + i, 0))] + cast_out_specs,
        out_shape=[jax.ShapeDtypeStruct((batch * tp, width), _BF16)] + cast_shapes,
        scratch_shapes=[pltpu.VMEM((tq, width), _F32), pltpu.VMEM((tq, width), _F32)],
        compiler_params=_params(("arbitrary", "arbitrary")),
        name="attn_prompt",
    )(pb, pb, pb, uo, g_attn, *[w for w, _ in cast])
    return outs[0], outs[1:]


def _attn_sample_kernel(q_ref, kn_ref, vn_ref, ck_hbm, cv_hbm, uos_ref, uoc_ref, g_ref, *rest,
                        n_heads, scale, ds, past, stream0, cast_every):
    n_cast = len(cast_every)
    cast_in, o_ref, cast_out = rest[:n_cast], rest[n_cast], rest[n_cast + 1:2 * n_cast + 1]
    acc_ref, car_ref, kbuf, vbuf, sem = rest[2 * n_cast + 1:]
    tk = CACHE_TILE
    n_ct = past // tk
    rows = tk * n_heads
    b = pl.program_id(0)
    heads = _head_slices(n_heads)
    _run_casts(b, cast_in, cast_out, cast_every)

    def copies(stream, blk, slot):
        row0 = pl.multiple_of(((stream0 + stream) * past + blk * tk) * n_heads, rows)
        return (pltpu.make_async_copy(ck_hbm.at[pl.ds(row0, rows), :], kbuf.at[slot], sem.at[0, slot]),
                pltpu.make_async_copy(cv_hbm.at[pl.ds(row0, rows), :], vbuf.at[slot], sem.at[1, slot]))

    def start(stream, blk, slot):
        for c in copies(stream, blk, slot):
            c.start()

    def wait(stream, blk, slot):
        for c in copies(stream, blk, slot):
            c.wait()

    newest = n_ct - 1
    par = lax.rem(b, 2)

    @pl.when(b == 0)
    def _():
        start(b, newest, par)

    @pl.when(b + 1 < pl.num_programs(0))
    def _():
        start(b + 1, newest, 1 - par)

    causal = (lax.broadcasted_iota(jnp.int32, (ds, ds), 1)
              < lax.broadcasted_iota(jnp.int32, (ds, ds), 0))
    qs = [q_ref[:, hs] for hs in heads]
    pvs, cars = _sb_tile(qs, [kn_ref[:, hs] for hs in heads], [vn_ref[:, hs] for hs in heads],
                         [None] * n_heads, uos_ref[...], causal, scale)
    for h, hs in enumerate(heads):
        acc_ref[:, hs] = pvs[h]
        car_ref[h] = cars[h]

    def cache_tile(slot):
        kslot, vslot = kbuf.at[slot], vbuf.at[slot]
        ks = [kslot[pl.ds(h, tk, stride=n_heads), :].astype(_BF16) for h in range(n_heads)]
        vs = [vslot[pl.ds(h, tk, stride=n_heads), :].astype(_BF16) for h in range(n_heads)]
        pvs, cars = _sb_tile([q_ref[:, hs] for hs in heads], ks, vs,
                             [car_ref[h] for h in range(n_heads)], uoc_ref[...], None, scale)
        for h, hs in enumerate(heads):
            acc_ref[:, hs] += pvs[h]
            car_ref[h] = cars[h]
        return (jnp.max(car_ref[...]) < LOG2_WEIGHT_FLOOR).astype(jnp.int32)

    wait(b, newest, par)
    done = cache_tile(par)

    if n_ct > 1:
        older = lambda blk: 2 + lax.rem(blk, 2)

        @pl.when(done == 0)
        def _():
            start(b, newest - 1, older(newest - 1))

        def cond(state):
            blk, done = state
            return jnp.logical_and(blk >= 0, done == 0)

        def body(state):
            blk, _ = state
            wait(b, blk, older(blk))

            @pl.when(blk > 0)
            def _():
                start(b, blk - 1, older(blk - 1))

            return blk - 1, cache_tile(older(blk))

        blk_end, _ = lax.while_loop(cond, body, (jnp.int32(newest - 1), done))

        @pl.when(jnp.logical_and(blk_end >= 0, blk_end < newest - 1))
        def _():
            wait(b, blk_end, older(blk_end))

    _head_norm_store(o_ref, acc_ref, g_ref, n_heads)


def _attn_sample(pb, cache_k, cache_v, g_attn, layer, row0, n_streams, ds, past, width, scale, cast):
    tk = CACHE_TILE
    n_heads = width // HEAD_DIM
    blk0 = row0 // ds
    uos = _suffix_matrix(ds, tk)
    uoc = _suffix_matrix(tk, tk)
    plans, cast_in_specs, cast_out_specs, cast_shapes = _cast_side_job(cast, n_streams, lambda b: b)
    outs = pl.pallas_call(
        functools.partial(_attn_sample_kernel, n_heads=n_heads, scale=scale, ds=ds, past=past,
                          stream0=layer * n_streams, cast_every=plans),
        grid=(n_streams,),
        in_specs=[pl.BlockSpec((ds, width), lambda b: (blk0 + b, 0)),
                  pl.BlockSpec((ds, width), lambda b: (blk0 + b, 1)),
                  pl.BlockSpec((ds, width), lambda b: (blk0 + b, 2)),
                  pl.BlockSpec(memory_space=pl.ANY),
                  pl.BlockSpec(memory_space=pl.ANY),
                  pl.BlockSpec((2 * ds, 2 * tk), lambda b: (0, 0)),
                  pl.BlockSpec((2 * tk, 2 * tk), lambda b: (0, 0)),
                  pl.BlockSpec((None, 1, width), lambda b: (layer, 0, 0))] + cast_in_specs,
        out_specs=[pl.BlockSpec((ds, width), lambda b: (b, 0))] + cast_out_specs,
        out_shape=[jax.ShapeDtypeStruct((n_streams * ds, width), _BF16)] + cast_shapes,
        scratch_shapes=[pltpu.VMEM((ds, width), _F32),
                        pltpu.VMEM((n_heads, ds, tk), _F32),
                        pltpu.VMEM((4, tk * n_heads, HEAD_DIM), _F32),
                        pltpu.VMEM((4, tk * n_heads, HEAD_DIM), _F32),
                        pltpu.SemaphoreType.DMA((2, 4))],
        compiler_params=_params(("arbitrary",)),
        name="attn_sample",
    )(pb, pb, pb, cache_k, cache_v, uos, uoc, g_attn, *[w for w, _ in cast])
    return outs[0], outs[1:]


def _conv_kernel(gb_ref, gc_ref, ci_ref, s0_ref, s1_ref, w_ref, g_ref, o_ref, tail_p_ref, tail_s_ref, ubuf_ref,
                 *, tp, t_real, n_prompt, ds, width):
    tc = CONV_TILE
    i = pl.program_id(0)

    @pl.when(i == 0)
    def _():
        ubuf_ref[0:8, :] = jnp.zeros((8, width), _F32)

    u = gc_ref[...] * ci_ref[...]
    ubuf_ref[8:8 + tc, :] = u

    for b in range(n_prompt // tp):
        g0 = b * tp + t_real - 8

        @pl.when(i == g0 // tc)
        def _(b=b, g0=g0):
            tail_p_ref[8 * b:8 * b + 8, :] = u[g0 % tc:g0 % tc + 8, :]

    @pl.when(i >= n_prompt // tc)
    def _():
        for s in range(tc // ds):
            tail_s_ref[8 * s:8 * s + 8, :] = u[ds * s + ds - 8:ds * s + ds, :]

    u1 = ubuf_ref[7:7 + tc, :]
    u2 = ubuf_ref[6:6 + tc, :]

    row = i * tc + lax.broadcasted_iota(jnp.int32, (tc, 1), 0)
    is_sample = row >= n_prompt
    pos = jnp.where(is_sample, lax.rem(row - n_prompt, ds), lax.rem(row, tp))
    first = pos == 0
    second = pos == 1
    seqs = tc // ds
    zero = jnp.zeros((tc, width), _F32)
    st0 = jnp.broadcast_to(s0_ref[...][:, None, :], (seqs, ds, width)).reshape(tc, width)
    st1 = jnp.broadcast_to(s1_ref[...][:, None, :], (seqs, ds, width)).reshape(tc, width)
    st0 = jnp.where(is_sample, st0, zero)
    st1 = jnp.where(is_sample, st1, zero)
    u1 = jnp.where(first, st1, u1)
    u2 = jnp.where(first, st0, jnp.where(second, st1, u2))
    y = u2 * w_ref[0:1, :] + u1 * w_ref[1:2, :] + u * w_ref[2:3, :]
    c = gb_ref[...] * y
    for gs in _head_slices(width // HEAD_DIM):
        o_ref[:, gs] = _rms(c[:, gs], g_ref[:, gs]).astype(o_ref.dtype)
    ubuf_ref[0:8, :] = u[tc - 8:tc, :]


def _conv(p, s0, s1, conv_w, g_conv, layer, tp, t_real, n_prompt, ds, width):
    tc = CONV_TILE
    n = p.shape[0]
    seqs = tc // ds
    first_sample_tile = n_prompt // tc
    batch = n_prompt // tp
    n_streams = (n - n_prompt) // ds
    assert t_real % 8 == 0 and ds % 8 == 0

    def state_map(i):
        return (jnp.maximum(i - first_sample_tile, 0), 0)

    return pl.pallas_call(
        functools.partial(_conv_kernel, tp=tp, t_real=t_real, n_prompt=n_prompt, ds=ds, width=width),
        grid=(n // tc,),
        in_specs=[pl.BlockSpec((tc, width), lambda i: (i, 0)),
                  pl.BlockSpec((tc, width), lambda i: (i, 1)),
                  pl.BlockSpec((tc, width), lambda i: (i, 2)),
                  pl.BlockSpec((seqs, width), state_map),
                  pl.BlockSpec((seqs, width), state_map),
                  pl.BlockSpec((None, CONV_K, width), lambda i: (layer, 0, 0)),
                  pl.BlockSpec((None, 1, width), lambda i: (layer, 0, 0))],
        out_specs=[pl.BlockSpec((tc, width), lambda i: (i, 0)),
                   pl.BlockSpec((batch * 8, width), lambda i: (0, 0)),
                   pl.BlockSpec((seqs * 8, width), state_map)],
        out_shape=[jax.ShapeDtypeStruct((n, width), _BF16),
                   jax.ShapeDtypeStruct((batch * 8, width), _F32),
                   jax.ShapeDtypeStruct((n_streams * 8, width), _F32)],
        scratch_shapes=[pltpu.VMEM((8 + tc, width), _F32)],
        compiler_params=_params(("arbitrary",)),
        name="gated_conv",
    )(p, p, p, s0, s1, conv_w, g_conv)


def _outproj_kernel(ap_ref, as_ref, c_ref, w_ref, h_ref, gpost_ref, gnext_ref, ho_ref, xo_ref, a_buf,
                    *, width, n_prompt, n_tiles):
    i = pl.program_id(0)
    tm = a_buf.shape[0]
    first_mixed = n_prompt // tm

    @pl.when(i < first_mixed)
    def _():
        a_buf[...] = ap_ref[...]

    for t in range(first_mixed, n_tiles):
        p_rows = max(0, n_prompt - t * tm)
        s0 = t * tm + p_rows - n_prompt

        @pl.when(i == t)
        def _(p_rows=p_rows, s0=s0):
            if p_rows:
                a_buf[0:p_rows, :] = ap_ref[0:p_rows, :]
            a_buf[p_rows:tm, :] = as_ref[s0:s0 + tm - p_rows, :]

    y = (jnp.dot(a_buf[...], w_ref[0:width, :], preferred_element_type=_F32)
         + jnp.dot(c_ref[...], w_ref[width:2 * width, :], preferred_element_type=_F32))
    h_new = h_ref[...] + _rms(y, gpost_ref[...])
    ho_ref[...] = h_new
    xo_ref[...] = _rms(h_new, gnext_ref[...]).astype(xo_ref.dtype)


def _outproj(attn_p, attn_s, conv_n, w_out, h, g_post, g_next, layer, tm):
    n, d = h.shape
    n_prompt, width = attn_p.shape
    n_tiles = n // tm
    last_prompt_tile = (n_prompt - 1) // tm
    assert n_prompt % 16 == 0 and tm % 16 == 0
    gain = pl.BlockSpec((None, 1, d), lambda i: (layer, 0, 0))
    return pl.pallas_call(
        functools.partial(_outproj_kernel, width=width, n_prompt=n_prompt, n_tiles=n_tiles),
        grid=(n_tiles,),
        in_specs=[pl.BlockSpec((tm, width), lambda i: (jnp.minimum(i, last_prompt_tile), 0)),
                  pl.BlockSpec(attn_s.shape, lambda i: (0, 0)),
                  pl.BlockSpec((tm, width), lambda i: (i, 0)),
                  pl.BlockSpec((2 * width, d), lambda i: (0, 0)),
                  pl.BlockSpec((tm, d), lambda i: (i, 0)),
                  gain, gain],
        out_specs=[pl.BlockSpec((tm, d), lambda i: (i, 0)),
                   pl.BlockSpec((tm, d), lambda i: (i, 0))],
        out_shape=[jax.ShapeDtypeStruct((n, d), _F32),
                   jax.ShapeDtypeStruct((n, d), _BF16)],
        scratch_shapes=[pltpu.VMEM((tm, width), _BF16)],
        compiler_params=_params(("arbitrary",)),
        name="out_proj",
    )(attn_p, attn_s, conv_n, w_out, h, g_post, g_next)


def _ffn_kernel(x_ref, wg_hbm, wu_hbm, wd_hbm, h_ref, gpost_ref, gnext_ref, ho_ref, xo_ref,
                wg_buf, wu_buf, wd_buf, sem, *, tf, n_chunks):
    i = pl.program_id(0)
    n_total = pl.num_programs(0) * n_chunks

    def copies(c, slot):
        col = pl.multiple_of(c * tf, tf)
        return (pltpu.make_async_copy(wg_hbm.at[:, pl.ds(col, tf)], wg_buf.at[slot], sem.at[0, slot]),
                pltpu.make_async_copy(wu_hbm.at[:, pl.ds(col, tf)], wu_buf.at[slot], sem.at[1, slot]),
                pltpu.make_async_copy(wd_hbm.at[pl.ds(col, tf), :], wd_buf.at[slot], sem.at[2, slot]))

    @pl.when(i == 0)
    def _():
        for cp in copies(0, 0):
            cp.start()

    ho_ref[...] = jnp.zeros_like(ho_ref)
    tm = x_ref.shape[0]
    split = (tm // 2 + 15) // 16 * 16
    groups = [slice(0, split), slice(split, tm)]

    def chunk(c, carry):
        g = i * n_chunks + c
        slot = lax.rem(g, 2)
        for cp in copies(c, slot):
            cp.wait()

        @pl.when(g + 1 < n_total)
        def _():
            for cp in copies(lax.rem(c + 1, n_chunks), 1 - slot):
                cp.start()

        wg, wu, wd = wg_buf[slot], wu_buf[slot], wd_buf[slot]
        gate_up = [(jnp.dot(x_ref[rs, :], wg, preferred_element_type=_F32),
                    jnp.dot(x_ref[rs, :], wu, preferred_element_type=_F32)) for rs in groups]
        for rs, (gate, up) in zip(groups, gate_up):
            act = (gate * jax.nn.sigmoid(gate) * up).astype(_BF16)
            ho_ref[rs, :] += jnp.dot(act, wd, preferred_element_type=_F32)
        return carry

    lax.fori_loop(0, n_chunks, chunk, 0)
    h_new = h_ref[...] + _rms(ho_ref[...], gpost_ref[...])
    ho_ref[...] = h_new
    xo_ref[...] = _rms(h_new, gnext_ref[...]).astype(xo_ref.dtype)


def _ffn(xf, w_gate, w_up, w_down, h, g_post, g_next, layer, next_layer, tm):
    n, d = h.shape
    d_ff = w_gate.shape[1]
    tf = FF_TILE
    return pl.pallas_call(
        functools.partial(_ffn_kernel, tf=tf, n_chunks=d_ff // tf),
        grid=(n // tm,),
        in_specs=[pl.BlockSpec((tm, d), lambda i: (i, 0)),
                  pl.BlockSpec(memory_space=pl.ANY),
                  pl.BlockSpec(memory_space=pl.ANY),
                  pl.BlockSpec(memory_space=pl.ANY),
                  pl.BlockSpec((tm, d), lambda i: (i, 0)),
                  pl.BlockSpec((None, 1, d), lambda i: (layer, 0, 0)),
                  pl.BlockSpec((None, 1, d), lambda i: (next_layer, 0, 0))],
        out_specs=[pl.BlockSpec((tm, d), lambda i: (i, 0)),
                   pl.BlockSpec((tm, d), lambda i: (i, 0))],
        out_shape=[jax.ShapeDtypeStruct((n, d), _F32),
                   jax.ShapeDtypeStruct((n, d), _BF16)],
        scratch_shapes=[pltpu.VMEM((2, d, tf), _BF16), pltpu.VMEM((2, d, tf), _BF16),
                        pltpu.VMEM((2, tf, d), _BF16), pltpu.SemaphoreType.DMA((3, 2))],
        compiler_params=_params(("arbitrary",)),
        name="swiglu_ffn",
    )(xf, w_gate, w_up, w_down, h, g_post, g_next)


def kernel(x_prompt, x_sample, cache_k, cache_v, state_conv, meta, w_in, w_out, conv_w, g_pre_mix, g_post_mix,
           g_attn_out, g_conv_out, g_pre_ffn, g_post_ffn, w_gate, w_up, w_down):
    batch, seq, d = x_prompt.shape
    n_streams, ds, _ = x_sample.shape
    depth = w_in.shape[0]
    width = w_out.shape[1] // 2
    n_heads = width // HEAD_DIM
    n_meta = meta.shape[0]
    past = cache_k.shape[2]
    t_real = n_meta + seq
    tp = -(-t_real // ATTN_TILE) * ATTN_TILE
    n_prompt = batch * tp
    n_sample = n_streams * ds
    n_tok = n_prompt + n_sample
    assert n_prompt % CONV_TILE == 0 and n_sample % CONV_TILE == 0 and CONV_TILE % ds == 0
    assert tp % ds == 0 and past % CACHE_TILE == 0 and ds >= CONV_K - 1
    assert w_gate.shape[2] % FF_TILE == 0 and cache_k.shape[3:] == (n_heads, HEAD_DIM)
    tm = _pick_tile(n_tok, ROW_TILE_TARGET)
    tm_in = _pick_tile(n_tok, IN_ROW_TILE_TARGET)
    scale = HEAD_DIM ** -0.5

    pad = jnp.zeros((tp - t_real, d), _F32)
    pieces = [piece for b in range(batch) for piece in (meta.astype(_F32), x_prompt[b], pad)]
    h = jnp.concatenate(pieces + [x_sample.reshape(n_sample, d)], axis=0)

    rows = lambda a: a.reshape(depth, 1, -1)
    g_pre_mix, g_post_mix, g_pre_ffn, g_post_ffn = rows(g_pre_mix), rows(g_post_mix), rows(g_pre_ffn), rows(g_post_ffn)
    g_attn_out, g_conv_out = rows(g_attn_out), rows(g_conv_out)
    cache_k2 = cache_k.reshape(-1, HEAD_DIM)
    cache_v2 = cache_v.reshape(-1, HEAD_DIM)

    xn = _norm(h, g_pre_mix, 0, tm)
    w_in_b = w_in[0].astype(_BF16)
    k_l, v_l, cp_l, cs_l = [], [], [], []
    for l in range(depth):
        p, pb, k3, v3 = _inproj(xn, w_in_b, tm_in, width)
        attn_p, (w_gate_b, w_up_b, w_down_b) = _attn_prompt(
            pb, g_attn_out, l, batch, tp, width, scale, [(w_gate, l), (w_up, l), (w_down, l)])
        attn_s, casted = _attn_sample(
            pb, cache_k2, cache_v2, g_attn_out, l, n_prompt, n_streams, ds, past, width, scale,
            [(w_out, l)] + ([(w_in, l + 1)] if l + 1 < depth else []))
        w_out_b = casted[0]
        w_in_b = casted[1] if l + 1 < depth else None
        conv_n, tail_p, tail_s = _conv(p, state_conv[l, :, 0, :], state_conv[l, :, 1, :], conv_w, g_conv_out, l,
                                       tp, t_real, n_prompt, ds, width)
        h, xf = _outproj(attn_p, attn_s, conv_n, w_out_b, h, g_post_mix, g_pre_ffn, l, tm)
        h, xn = _ffn(xf, w_gate_b, w_up_b, w_down_b, h, g_post_ffn, g_pre_mix, l, min(l + 1, depth - 1), tm)
        k_l.append(k3.reshape(n_tok, n_heads, HEAD_DIM))
        v_l.append(v3.reshape(n_tok, n_heads, HEAD_DIM))
        cp_l.append(tail_p.reshape(batch, 8, width)[:, 8 - (CONV_K - 1):])
        cs_l.append(tail_s.reshape(n_streams, 8, width)[:, 8 - (CONV_K - 1):])

    def prompt_part(xs, lo, hi):
        rows_ = jnp.concatenate([x[b * tp + lo:b * tp + hi] for x in xs for b in range(batch)], axis=0)
        return rows_.reshape((len(xs), batch, hi - lo) + rows_.shape[1:])

    def sample_part(xs, lo):
        return jnp.stack([x[n_prompt:].reshape((n_streams, ds) + x.shape[1:])[:, lo:] for x in xs])

    y_prompt = prompt_part([h], n_meta, t_real)[0]
    y_sample = h[n_prompt:].reshape(n_streams, ds, d)
    return (y_prompt, y_sample,
            prompt_part(k_l, 0, t_real), prompt_part(v_l, 0, t_real), jnp.stack(cp_l),
            sample_part(k_l, 0), sample_part(v_l, 0), jnp.stack(cs_l))
```

```python
import functools

import jax
import jax.numpy as jnp
from jax import lax
from jax.experimental import pallas as pl
from jax.experimental.pallas import tpu as pltpu

HEAD_DIM = 128
CONV_K = 3
EPS = 1e-6
ATTN_TILE = 128
CACHE_TILE = 256
CONV_TILE = 256
ROW_TILE_TARGET = 600
IN_ROW_TILE_TARGET = 1200
IN_COL_TILE = 512
FF_TILE = 512
VMEM_LIMIT = 56 * 1024 * 1024
LOG2_E = 1.4426950408889634
LOG2_WEIGHT_FLOOR = -152.0

_F32 = jnp.float32
_BF16 = jnp.bfloat16
_NT = (((1,), (1,)), ((), ()))


def _pick_tile(n, target, mult=16):
    best = None
    for t in range(mult, target + 1, mult):
        if n % t == 0:
            best = t
    assert best is not None, (n, target)
    return best


def _params(sem):
    return pltpu.CompilerParams(dimension_semantics=sem, vmem_limit_bytes=VMEM_LIMIT)


def _rms(x, g):
    ms = jnp.mean(x * x, axis=-1, keepdims=True)
    return x * lax.rsqrt(ms + EPS) * g


def _head_slices(n_heads):
    return [slice(h * HEAD_DIM, (h + 1) * HEAD_DIM) for h in range(n_heads)]


def _norm_kernel(h_ref, g_ref, o_ref):
    o_ref[...] = _rms(h_ref[...], g_ref[...]).astype(o_ref.dtype)


def _norm(h, g, layer, tm):
    n, d = h.shape
    return pl.pallas_call(
        _norm_kernel,
        grid=(n // tm,),
        in_specs=[pl.BlockSpec((tm, d), lambda i: (i, 0)),
                  pl.BlockSpec((None, 1, d), lambda i: (layer, 0, 0))],
        out_specs=pl.BlockSpec((tm, d), lambda i: (i, 0)),
        out_shape=jax.ShapeDtypeStruct((n, d), _BF16),
        compiler_params=_params(("arbitrary",)),
        name="pre_norm",
    )(h, g)


def _inproj_kernel(x_ref, w_ref, p_ref, pb_ref, k_ref, v_ref, *, n_heads, tm, wpk):
    j = pl.program_id(1)
    acc = jnp.dot(x_ref[...], w_ref[...], preferred_element_type=_F32)
    p_ref[...] = acc
    pb_ref[...] = acc.astype(_BF16)

    heads_per_tile = n_heads // wpk
    for ref, first in ((k_ref, wpk), (v_ref, 2 * wpk)):
        for part in range(wpk):
            @pl.when(j == first + part)
            def _(ref=ref, part=part):
                for hh, hs in enumerate(_head_slices(heads_per_tile)):
                    ref[pl.ds(part * heads_per_tile + hh, tm, stride=n_heads), :] = p_ref[:, hs]


def _inproj(xn, w, tm, width):
    n, d = xn.shape
    n_heads = width // HEAD_DIM
    tn = w.shape[2]
    wpk = width // tn
    kv_shape = jax.ShapeDtypeStruct((n * n_heads, HEAD_DIM), _F32)
    kv_spec = pl.BlockSpec((tm * n_heads, HEAD_DIM), lambda i, j: (i, 0))
    return pl.pallas_call(
        functools.partial(_inproj_kernel, n_heads=n_heads, tm=tm, wpk=wpk),
        grid=(n // tm, 6 * wpk),
        in_specs=[pl.BlockSpec((tm, d), lambda i, j: (i, 0)),
                  pl.BlockSpec((None, d, tn), lambda i, j: (j, 0, 0))],
        out_specs=[pl.BlockSpec((tm, tn), lambda i, j: (i, jnp.where(j < 3 * wpk, 3 * wpk, j - 3 * wpk))),
                   pl.BlockSpec((tm, tn), lambda i, j: (i, jnp.minimum(j, 3 * wpk))),
                   kv_spec, kv_spec],
        out_shape=[jax.ShapeDtypeStruct((n, 3 * width + tn), _F32),
                   jax.ShapeDtypeStruct((n, 3 * width + tn), _BF16),
                   kv_shape, kv_shape],
        compiler_params=_params(("arbitrary", "arbitrary")),
        name="in_proj",
    )(xn, w)


def _log2_sigmoids(z2):
    lsp = jnp.minimum(z2, 0.0) - jnp.log2(1.0 + jnp.exp2(-jnp.abs(z2)))
    return lsp, lsp - z2


def _suffix_sums(l_mats, uo):
    tq, tk = l_mats[0].shape
    sub, wc = uo.shape[0] // 2, uo.shape[1] // 2
    his = [l.astype(_BF16) for l in l_mats]
    los = [(l - hi.astype(_F32)).astype(_BF16) for l, hi in zip(l_mats, his)]
    if sub % HEAD_DIM:
        assert tk == sub
        top = uo[:tk]
        rs = [jnp.dot(hi, top, preferred_element_type=_F32) + jnp.dot(lo, top, preferred_element_type=_F32)
              for hi, lo in zip(his, los)]
        return [(r[:, :tk], r[:, wc:]) for r in rs]
    n_sub = tk // sub
    cols = [slice(b * sub, (b + 1) * sub) for b in range(n_sub)]
    lhs = jnp.concatenate([jnp.concatenate([hi[:, cs], lo[:, cs]], axis=1)
                           for hi, lo in zip(his, los) for cs in cols], axis=0)
    r = jnp.dot(lhs, uo, preferred_element_type=_F32)
    out = []
    for h in range(len(l_mats)):
        newer, afters = None, [None] * n_sub
        for b in reversed(range(n_sub)):
            r_hb = r[(h * n_sub + b) * tq:(h * n_sub + b + 1) * tq]
            afters[b] = r_hb[:, :sub] if newer is None else r_hb[:, :sub] + newer[:, :sub]
            newer = r_hb[:, wc:] if newer is None else newer + r_hb[:, wc:]
        out.append((afters[0] if n_sub == 1 else jnp.concatenate(afters, axis=1), newer))
    return out


def _suffix_matrix(rows, width):
    j = lax.rem(lax.broadcasted_iota(jnp.int32, (2 * rows, 2 * width), 0), rows)
    s = lax.broadcasted_iota(jnp.int32, (2 * rows, 2 * width), 1)
    return jnp.where((s >= width) | (j > s), 1.0, 0.0).astype(_BF16)


def _sb_tile(qs, ks, vs, cars, uo, vis, scale):
    tk = ks[0].shape[0]
    zs = [lax.dot_general(q, k, _NT, preferred_element_type=_F32) * (scale * LOG2_E) for q, k in zip(qs, ks)]
    lsps, lsns = [], []
    for z in zs:
        lsp, lsn = _log2_sigmoids(z)
        if vis is not None:
            lsn = jnp.where(vis, lsn, 0.0)
        lsps.append(lsp)
        lsns.append(lsn)
    sums = _suffix_sums(lsns, uo)
    pvs, new_cars = [], []
    for lsp, (after, total), v, car in zip(lsps, sums, vs, cars):
        x = lsp + after
        if car is not None:
            cw = car.shape[1]
            x = x + (car[:, :tk] if cw >= tk else jnp.concatenate([car] * (tk // cw), axis=1))
        w = jnp.exp2(x)
        if vis is not None:
            w = jnp.where(vis, w, 0.0)
        pvs.append(jnp.dot(w.astype(_BF16), v, preferred_element_type=_F32))
        new_cars.append(total if car is None else car + total[:, :cw])
    return pvs, new_cars


def _head_norm_store(o_ref, acc_ref, g_ref, n_heads):
    for hs in _head_slices(n_heads):
        o_ref[:, hs] = _rms(acc_ref[:, hs], g_ref[:, hs]).astype(o_ref.dtype)


def _attn_prompt_kernel(q_ref, k_ref, v_ref, uo_ref, g_ref, *rest, n_heads, scale, cast_every):
    n_cast = len(cast_every)
    cast_in, o_ref, cast_out = rest[:n_cast], rest[n_cast], rest[n_cast + 1:2 * n_cast + 1]
    acc_ref, car_ref = rest[2 * n_cast + 1:]
    tq = ATTN_TILE
    i = pl.program_id(1)
    heads = _head_slices(n_heads)

    _run_casts(pl.program_id(0) * pl.num_programs(1) + i, cast_in, cast_out, cast_every)

    def tile(blk, n_blk, vis, first):
        k0 = pl.multiple_of(blk * tq, tq)
        qs = [q_ref[:, hs] for hs in heads]
        ks = [k_ref[pl.ds(k0, n_blk * tq), hs] for hs in heads]
        vs = [v_ref[pl.ds(k0, n_blk * tq), hs] for hs in heads]
        cars = [None if first else car_ref[:, hs] for hs in heads]
        pvs, new_cars = _sb_tile(qs, ks, vs, cars, uo_ref[...], vis, scale)
        for hs, pv, car in zip(heads, pvs, new_cars):
            if first:
                acc_ref[:, hs] = pv
            else:
                acc_ref[:, hs] += pv
            car_ref[:, hs] = car

    def causal_last(n_blk):
        off = (n_blk - 1) * tq
        return (lax.broadcasted_iota(jnp.int32, (tq, n_blk * tq), 1) - off
                < lax.broadcasted_iota(jnp.int32, (tq, n_blk * tq), 0))

    @pl.when(i >= 2)
    def _():
        tile(i - 2, 3, causal_last(3), True)

    @pl.when(i < 2)
    def _():
        tile(i, 1, causal_last(1), True)

    def all_rows_done():
        return (jnp.max(car_ref[...]) < LOG2_WEIGHT_FLOOR).astype(jnp.int32)

    def cond(state):
        left, done = state
        return jnp.logical_and(left >= 2, done == 0)

    def body(state):
        left, _ = state
        tile(left - 2, 2, None, False)
        return left - 2, all_rows_done()

    left, done = lax.while_loop(cond, body, (jnp.where(i >= 2, i - 2, i), all_rows_done()))

    @pl.when(jnp.logical_and(left == 1, done == 0))
    def _():
        tile(0, 1, None, False)

    _head_norm_store(o_ref, acc_ref, g_ref, n_heads)


def _cast_plan(rows, n_steps):
    n_blocks = 1
    while n_blocks * 2 <= n_steps and rows % (n_blocks * 2) == 0 and (rows // (n_blocks * 2)) % 16 == 0:
        n_blocks *= 2
    every = 1
    while every * 2 * n_blocks <= n_steps:
        every *= 2
    return every, n_blocks


def _cast_side_job(cast, n_steps, step_of):
    plans = tuple(_cast_plan(w.shape[1], n_steps) + (ct,) for w, _, ct in cast)

    def specs(w, lyr, plan):
        every, n_blocks, ct = plan
        rows, cols = w.shape[1:]
        blk = rows // n_blocks
        pick = lambda *idx: jnp.minimum(step_of(*idx) // every, n_blocks - 1)
        src = pl.BlockSpec((None, blk, cols), lambda *idx: (lyr, pick(*idx), 0))
        if ct is None:
            return (src, pl.BlockSpec((blk, cols), lambda *idx: (pick(*idx), 0)),
                    jax.ShapeDtypeStruct((rows, cols), _BF16))
        return (src, pl.BlockSpec((cols // ct, blk, ct), lambda *idx: (0, pick(*idx), 0)),
                jax.ShapeDtypeStruct((cols // ct, rows, ct), _BF16))

    all_specs = [specs(w, lyr, plan) for (w, lyr, _), plan in zip(cast, plans)]
    return plans, [s[0] for s in all_specs], [s[1] for s in all_specs], [s[2] for s in all_specs]


def _run_casts(step, cast_in, cast_out, plans):
    for src, dst, (every, n_blocks, ct) in zip(cast_in, cast_out, plans):
        @pl.when(jnp.logical_and(step < every * n_blocks, lax.rem(step, every) == 0))
        def _(src=src, dst=dst, ct=ct):
            if ct is None:
                dst[...] = src[...].astype(_BF16)
            else:
                for t in range(dst.shape[0]):
                    dst[t] = src[:, t * ct:(t + 1) * ct].astype(_BF16)


def _attn_prompt(pb, g_attn, layer, batch, tp, width, scale, cast):
    tq = ATTN_TILE
    nq = tp // tq
    n_heads = width // HEAD_DIM
    uo = _suffix_matrix(tq, tq)
    plans, cast_in_specs, cast_out_specs, cast_shapes = _cast_side_job(cast, batch * nq, lambda b, i: b * nq + i)
    outs = pl.pallas_call(
        functools.partial(_attn_prompt_kernel, n_heads=n_heads, scale=scale, cast_every=plans),
        grid=(batch, nq),
        in_specs=[pl.BlockSpec((tq, width), lambda b, i: (b * nq + i, 0)),
                  pl.BlockSpec((tp, width), lambda b, i: (b, 1)),
                  pl.BlockSpec((tp, width), lambda b, i: (b, 2)),
                  pl.BlockSpec((2 * tq, 2 * tq), lambda b, i: (0, 0)),
                  pl.BlockSpec((None, 1, width), lambda b, i: (layer, 0, 0))] + cast_in_specs,
        out_specs=[pl.BlockSpec((tq, width), lambda b, i: (b * nq + i, 0))] + cast_out_specs,
        out_shape=[jax.ShapeDtypeStruct((batch * tp, width), _BF16)] + cast_shapes,
        scratch_shapes=[pltpu.VMEM((tq, width), _F32), pltpu.VMEM((tq, width), _F32)],
        compiler_params=_params(("arbitrary", "arbitrary")),
        name="attn_prompt",
    )(pb, pb, pb, uo, g_attn, *[c[0] for c in cast])
    return outs[0], outs[1:]


def _attn_sample_kernel(q_ref, kn_ref, vn_ref, ck_hbm, cv_hbm, uos_ref, uoc_ref, g_ref, *rest,
                        n_heads, scale, ds, past, stream0, cast_every):
    n_cast = len(cast_every)
    cast_in, o_ref, cast_out = rest[:n_cast], rest[n_cast], rest[n_cast + 1:2 * n_cast + 1]
    acc_ref, car_ref, kbuf, vbuf, sem = rest[2 * n_cast + 1:]
    tk = CACHE_TILE
    n_ct = past // tk
    rows = tk * n_heads
    b = pl.program_id(0)
    heads = _head_slices(n_heads)
    _run_casts(b, cast_in, cast_out, cast_every)

    def copies(stream, blk, slot):
        row0 = pl.multiple_of(((stream0 + stream) * past + blk * tk) * n_heads, rows)
        return (pltpu.make_async_copy(ck_hbm.at[pl.ds(row0, rows), :], kbuf.at[slot], sem.at[0, slot]),
                pltpu.make_async_copy(cv_hbm.at[pl.ds(row0, rows), :], vbuf.at[slot], sem.at[1, slot]))

    def start(stream, blk, slot):
        for c in copies(stream, blk, slot):
            c.start()

    def wait(stream, blk, slot):
        for c in copies(stream, blk, slot):
            c.wait()

    newest = n_ct - 1
    par = lax.rem(b, 2)

    @pl.when(b == 0)
    def _():
        start(b, newest, par)

    @pl.when(b + 1 < pl.num_programs(0))
    def _():
        start(b + 1, newest, 1 - par)

    causal = (lax.broadcasted_iota(jnp.int32, (ds, ds), 1)
              < lax.broadcasted_iota(jnp.int32, (ds, ds), 0))
    qs = [q_ref[:, hs] for hs in heads]
    pvs, cars = _sb_tile(qs, [kn_ref[:, hs] for hs in heads], [vn_ref[:, hs] for hs in heads],
                         [None] * n_heads, uos_ref[...], causal, scale)
    for h, hs in enumerate(heads):
        acc_ref[:, hs] = pvs[h]
        car_ref[h] = cars[h]

    def cache_tile(slot):
        kslot, vslot = kbuf.at[slot], vbuf.at[slot]
        ks = [kslot[pl.ds(h, tk, stride=n_heads), :].astype(_BF16) for h in range(n_heads)]
        vs = [vslot[pl.ds(h, tk, stride=n_heads), :].astype(_BF16) for h in range(n_heads)]
        pvs, cars = _sb_tile([q_ref[:, hs] for hs in heads], ks, vs,
                             [car_ref[h] for h in range(n_heads)], uoc_ref[...], None, scale)
        for h, hs in enumerate(heads):
            acc_ref[:, hs] += pvs[h]
            car_ref[h] = cars[h]
        return (jnp.max(car_ref[...]) < LOG2_WEIGHT_FLOOR).astype(jnp.int32)

    wait(b, newest, par)
    done = cache_tile(par)

    if n_ct > 1:
        older = lambda blk: 2 + lax.rem(blk, 2)

        @pl.when(done == 0)
        def _():
            start(b, newest - 1, older(newest - 1))

        def cond(state):
            blk, done = state
            return jnp.logical_and(blk >= 0, done == 0)

        def body(state):
            blk, _ = state
            wait(b, blk, older(blk))

            @pl.when(blk > 0)
            def _():
                start(b, blk - 1, older(blk - 1))

            return blk - 1, cache_tile(older(blk))

        blk_end, _ = lax.while_loop(cond, body, (jnp.int32(newest - 1), done))

        @pl.when(jnp.logical_and(blk_end >= 0, blk_end < newest - 1))
        def _():
            wait(b, blk_end, older(blk_end))

    _head_norm_store(o_ref, acc_ref, g_ref, n_heads)


def _attn_sample(pb, cache_k, cache_v, g_attn, layer, row0, n_streams, ds, past, width, scale, cast):
    tk = CACHE_TILE
    n_heads = width // HEAD_DIM
    blk0 = row0 // ds
    uos = _suffix_matrix(ds, tk)
    uoc = _suffix_matrix(tk, tk)
    plans, cast_in_specs, cast_out_specs, cast_shapes = _cast_side_job(cast, n_streams, lambda b: b)
    outs = pl.pallas_call(
        functools.partial(_attn_sample_kernel, n_heads=n_heads, scale=scale, ds=ds, past=past,
                          stream0=layer * n_streams, cast_every=plans),
        grid=(n_streams,),
        in_specs=[pl.BlockSpec((ds, width), lambda b: (blk0 + b, 0)),
                  pl.BlockSpec((ds, width), lambda b: (blk0 + b, 1)),
                  pl.BlockSpec((ds, width), lambda b: (blk0 + b, 2)),
                  pl.BlockSpec(memory_space=pl.ANY),
                  pl.BlockSpec(memory_space=pl.ANY),
                  pl.BlockSpec((2 * ds, 2 * tk), lambda b: (0, 0)),
                  pl.BlockSpec((2 * tk, 2 * tk), lambda b: (0, 0)),
                  pl.BlockSpec((None, 1, width), lambda b: (layer, 0, 0))] + cast_in_specs,
        out_specs=[pl.BlockSpec((ds, width), lambda b: (b, 0))] + cast_out_specs,
        out_shape=[jax.ShapeDtypeStruct((n_streams * ds, width), _BF16)] + cast_shapes,
        scratch_shapes=[pltpu.VMEM((ds, width), _F32),
                        pltpu.VMEM((n_heads, ds, tk), _F32),
                        pltpu.VMEM((4, tk * n_heads, HEAD_DIM), _F32),
                        pltpu.VMEM((4, tk * n_heads, HEAD_DIM), _F32),
                        pltpu.SemaphoreType.DMA((2, 4))],
        compiler_params=_params(("arbitrary",)),
        name="attn_sample",
    )(pb, pb, pb, cache_k, cache_v, uos, uoc, g_attn, *[c[0] for c in cast])
    return outs[0], outs[1:]


def _conv_kernel(gb_ref, gc_ref, ci_ref, s0_ref, s1_ref, w_ref, g_ref, o_ref, tail_p_ref, tail_s_ref, ubuf_ref,
                 *, tp, t_real, n_prompt, ds, width):
    tc = CONV_TILE
    i = pl.program_id(0)

    @pl.when(i == 0)
    def _():
        ubuf_ref[0:8, :] = jnp.zeros((8, width), _F32)

    u = gc_ref[...] * ci_ref[...]
    ubuf_ref[8:8 + tc, :] = u

    for b in range(n_prompt // tp):
        g0 = b * tp + t_real - 8

        @pl.when(i == g0 // tc)
        def _(b=b, g0=g0):
            tail_p_ref[8 * b:8 * b + 8, :] = u[g0 % tc:g0 % tc + 8, :]

    @pl.when(i >= n_prompt // tc)
    def _():
        for s in range(tc // ds):
            tail_s_ref[8 * s:8 * s + 8, :] = u[ds * s + ds - 8:ds * s + ds, :]

    u1 = ubuf_ref[7:7 + tc, :]
    u2 = ubuf_ref[6:6 + tc, :]

    row = i * tc + lax.broadcasted_iota(jnp.int32, (tc, 1), 0)
    is_sample = row >= n_prompt
    pos = jnp.where(is_sample, lax.rem(row - n_prompt, ds), lax.rem(row, tp))
    first = pos == 0
    second = pos == 1
    seqs = tc // ds
    zero = jnp.zeros((tc, width), _F32)
    st0 = jnp.broadcast_to(s0_ref[...][:, None, :], (seqs, ds, width)).reshape(tc, width)
    st1 = jnp.broadcast_to(s1_ref[...][:, None, :], (seqs, ds, width)).reshape(tc, width)
    st0 = jnp.where(is_sample, st0, zero)
    st1 = jnp.where(is_sample, st1, zero)
    u1 = jnp.where(first, st1, u1)
    u2 = jnp.where(first, st0, jnp.where(second, st1, u2))
    y = u2 * w_ref[0:1, :] + u1 * w_ref[1:2, :] + u * w_ref[2:3, :]
    c = gb_ref[...] * y
    for gs in _head_slices(width // HEAD_DIM):
        o_ref[:, gs] = _rms(c[:, gs], g_ref[:, gs]).astype(o_ref.dtype)
    ubuf_ref[0:8, :] = u[tc - 8:tc, :]


def _conv(p, s0, s1, conv_w, g_conv, layer, tp, t_real, n_prompt, ds, width):
    tc = CONV_TILE
    n = p.shape[0]
    seqs = tc // ds
    first_sample_tile = n_prompt // tc
    batch = n_prompt // tp
    n_streams = (n - n_prompt) // ds
    assert t_real % 8 == 0 and ds % 8 == 0

    def state_map(i):
        return (jnp.maximum(i - first_sample_tile, 0), 0)

    return pl.pallas_call(
        functools.partial(_conv_kernel, tp=tp, t_real=t_real, n_prompt=n_prompt, ds=ds, width=width),
        grid=(n // tc,),
        in_specs=[pl.BlockSpec((tc, width), lambda i: (i, 0)),
                  pl.BlockSpec((tc, width), lambda i: (i, 1)),
                  pl.BlockSpec((tc, width), lambda i: (i, 2)),
                  pl.BlockSpec((seqs, width), state_map),
                  pl.BlockSpec((seqs, width), state_map),
                  pl.BlockSpec((None, CONV_K, width), lambda i: (layer, 0, 0)),
                  pl.BlockSpec((None, 1, width), lambda i: (layer, 0, 0))],
        out_specs=[pl.BlockSpec((tc, width), lambda i: (i, 0)),
                   pl.BlockSpec((batch * 8, width), lambda i: (0, 0)),
                   pl.BlockSpec((seqs * 8, width), state_map)],
        out_shape=[jax.ShapeDtypeStruct((n, width), _BF16),
                   jax.ShapeDtypeStruct((batch * 8, width), _F32),
                   jax.ShapeDtypeStruct((n_streams * 8, width), _F32)],
        scratch_shapes=[pltpu.VMEM((8 + tc, width), _F32)],
        compiler_params=_params(("arbitrary",)),
        name="gated_conv",
    )(p, p, p, s0, s1, conv_w, g_conv)


def _outproj_kernel(ap_ref, as_ref, c_ref, w_ref, h_ref, gpost_ref, gnext_ref, ho_ref, xo_ref, a_buf,
                    *, width, n_prompt, n_tiles):
    i = pl.program_id(0)
    tm = a_buf.shape[0]
    first_mixed = n_prompt // tm

    @pl.when(i < first_mixed)
    def _():
        a_buf[...] = ap_ref[...]

    for t in range(first_mixed, n_tiles):
        p_rows = max(0, n_prompt - t * tm)
        s0 = t * tm + p_rows - n_prompt

        @pl.when(i == t)
        def _(p_rows=p_rows, s0=s0):
            if p_rows:
                a_buf[0:p_rows, :] = ap_ref[0:p_rows, :]
            a_buf[p_rows:tm, :] = as_ref[s0:s0 + tm - p_rows, :]

    y = (jnp.dot(a_buf[...], w_ref[0:width, :], preferred_element_type=_F32)
         + jnp.dot(c_ref[...], w_ref[width:2 * width, :], preferred_element_type=_F32))
    h_new = h_ref[...] + _rms(y, gpost_ref[...])
    ho_ref[...] = h_new
    xo_ref[...] = _rms(h_new, gnext_ref[...]).astype(xo_ref.dtype)


def _outproj(attn_p, attn_s, conv_n, w_out, h, g_post, g_next, layer, tm):
    n, d = h.shape
    n_prompt, width = attn_p.shape
    n_tiles = n // tm
    last_prompt_tile = (n_prompt - 1) // tm
    assert n_prompt % 16 == 0 and tm % 16 == 0
    gain = pl.BlockSpec((None, 1, d), lambda i: (layer, 0, 0))
    return pl.pallas_call(
        functools.partial(_outproj_kernel, width=width, n_prompt=n_prompt, n_tiles=n_tiles),
        grid=(n_tiles,),
        in_specs=[pl.BlockSpec((tm, width), lambda i: (jnp.minimum(i, last_prompt_tile), 0)),
                  pl.BlockSpec(attn_s.shape, lambda i: (0, 0)),
                  pl.BlockSpec((tm, width), lambda i: (i, 0)),
                  pl.BlockSpec((2 * width, d), lambda i: (0, 0)),
                  pl.BlockSpec((tm, d), lambda i: (i, 0)),
                  gain, gain],
        out_specs=[pl.BlockSpec((tm, d), lambda i: (i, 0)),
                   pl.BlockSpec((tm, d), lambda i: (i, 0))],
        out_shape=[jax.ShapeDtypeStruct((n, d), _F32),
                   jax.ShapeDtypeStruct((n, d), _BF16)],
        scratch_shapes=[pltpu.VMEM((tm, width), _BF16)],
        compiler_params=_params(("arbitrary",)),
        name="out_proj",
    )(attn_p, attn_s, conv_n, w_out, h, g_post, g_next)


def _ffn_kernel(x_ref, wg_ref, wu_ref, wd_ref, h_ref, gpost_ref, gnext_ref, ho_ref, xo_ref):
    c = pl.program_id(1)

    @pl.when(c == 0)
    def _():
        ho_ref[...] = jnp.zeros_like(ho_ref)

    tm = x_ref.shape[0]
    split = (tm // 2 + 15) // 16 * 16
    groups = [slice(0, split), slice(split, tm)]
    wg, wu, wd = wg_ref[...], wu_ref[...], wd_ref[...]
    gate_up = [(jnp.dot(x_ref[rs, :], wg, preferred_element_type=_F32),
                jnp.dot(x_ref[rs, :], wu, preferred_element_type=_F32)) for rs in groups]
    for rs, (gate, up) in zip(groups, gate_up):
        act = (gate * jax.nn.sigmoid(gate) * up).astype(_BF16)
        ho_ref[rs, :] += jnp.dot(act, wd, preferred_element_type=_F32)

    @pl.when(c == pl.num_programs(1) - 1)
    def _():
        h_new = h_ref[...] + _rms(ho_ref[...], gpost_ref[...])
        ho_ref[...] = h_new
        xo_ref[...] = _rms(h_new, gnext_ref[...]).astype(xo_ref.dtype)


def _ffn(xf, w_gate, w_up, w_down, h, g_post, g_next, layer, next_layer, tm):
    n, d = h.shape
    n_chunks, _, tf = w_gate.shape
    return pl.pallas_call(
        _ffn_kernel,
        grid=(n // tm, n_chunks),
        in_specs=[pl.BlockSpec((tm, d), lambda i, c: (i, 0)),
                  pl.BlockSpec((None, d, tf), lambda i, c: (c, 0, 0)),
                  pl.BlockSpec((None, d, tf), lambda i, c: (c, 0, 0)),
                  pl.BlockSpec((tf, d), lambda i, c: (c, 0)),
                  pl.BlockSpec((tm, d), lambda i, c: (i, 0)),
                  pl.BlockSpec((None, 1, d), lambda i, c: (layer, 0, 0)),
                  pl.BlockSpec((None, 1, d), lambda i, c: (next_layer, 0, 0))],
        out_specs=[pl.BlockSpec((tm, d), lambda i, c: (i, 0)),
                   pl.BlockSpec((tm, d), lambda i, c: (i, 0))],
        out_shape=[jax.ShapeDtypeStruct((n, d), _F32),
                   jax.ShapeDtypeStruct((n, d), _BF16)],
        compiler_params=_params(("arbitrary", "arbitrary")),
        name="swiglu_ffn",
    )(xf, w_gate, w_up, w_down, h, g_post, g_next)


def kernel(x_prompt, x_sample, cache_k, cache_v, state_conv, meta, w_in, w_out, conv_w, g_pre_mix, g_post_mix,
           g_attn_out, g_conv_out, g_pre_ffn, g_post_ffn, w_gate, w_up, w_down):
    batch, seq, d = x_prompt.shape
    n_streams, ds, _ = x_sample.shape
    depth = w_in.shape[0]
    width = w_out.shape[1] // 2
    n_heads = width // HEAD_DIM
    n_meta = meta.shape[0]
    past = cache_k.shape[2]
    t_real = n_meta + seq
    tp = -(-t_real // ATTN_TILE) * ATTN_TILE
    n_prompt = batch * tp
    n_sample = n_streams * ds
    n_tok = n_prompt + n_sample
    assert n_prompt % CONV_TILE == 0 and n_sample % CONV_TILE == 0 and CONV_TILE % ds == 0
    assert tp % ds == 0 and past % CACHE_TILE == 0 and ds >= CONV_K - 1
    assert w_gate.shape[2] % FF_TILE == 0 and cache_k.shape[3:] == (n_heads, HEAD_DIM)
    tm = _pick_tile(n_tok, ROW_TILE_TARGET)
    tm_in = _pick_tile(n_tok, IN_ROW_TILE_TARGET)
    scale = HEAD_DIM ** -0.5

    pad = jnp.zeros((tp - t_real, d), _F32)
    pieces = [piece for b in range(batch) for piece in (meta.astype(_F32), x_prompt[b], pad)]
    h = jnp.concatenate(pieces + [x_sample.reshape(n_sample, d)], axis=0)

    rows = lambda a: a.reshape(depth, 1, -1)
    g_pre_mix, g_post_mix, g_pre_ffn, g_post_ffn = rows(g_pre_mix), rows(g_post_mix), rows(g_pre_ffn), rows(g_post_ffn)
    g_attn_out, g_conv_out = rows(g_attn_out), rows(g_conv_out)
    cache_k2 = cache_k.reshape(-1, HEAD_DIM)
    cache_v2 = cache_v.reshape(-1, HEAD_DIM)

    xn = _norm(h, g_pre_mix, 0, tm)
    tn = min(IN_COL_TILE, width)
    w_in_b = w_in[0].reshape(d, -1, tn).transpose(1, 0, 2).astype(_BF16)
    k_l, v_l, cp_l, cs_l = [], [], [], []
    for l in range(depth):
        p, pb, k3, v3 = _inproj(xn, w_in_b, tm_in, width)
        attn_p, (w_gate_b, w_up_b, w_down_b) = _attn_prompt(
            pb, g_attn_out, l, batch, tp, width, scale,
            [(w_gate, l, FF_TILE), (w_up, l, FF_TILE), (w_down, l, None)])
        attn_s, casted = _attn_sample(
            pb, cache_k2, cache_v2, g_attn_out, l, n_prompt, n_streams, ds, past, width, scale,
            [(w_out, l, None)] + ([(w_in, l + 1, tn)] if l + 1 < depth else []))
        w_out_b = casted[0]
        w_in_b = casted[1] if l + 1 < depth else None
        conv_n, tail_p, tail_s = _conv(p, state_conv[l, :, 0, :], state_conv[l, :, 1, :], conv_w, g_conv_out, l,
                                       tp, t_real, n_prompt, ds, width)
        h, xf = _outproj(attn_p, attn_s, conv_n, w_out_b, h, g_post_mix, g_pre_ffn, l, tm)
        h, xn = _ffn(xf, w_gate_b, w_up_b, w_down_b, h, g_post_ffn, g_pre_mix, l, min(l + 1, depth - 1), tm)
        k_l.append(k3.reshape(n_tok, n_heads, HEAD_DIM))
        v_l.append(v3.reshape(n_tok, n_heads, HEAD_DIM))
        cp_l.append(tail_p.reshape(batch, 8, width)[:, 8 - (CONV_K - 1):])
        cs_l.append(tail_s.reshape(n_streams, 8, width)[:, 8 - (CONV_K - 1):])

    def prompt_part(xs, lo, hi):
        rows_ = jnp.concatenate([x[b * tp + lo:b * tp + hi] for x in xs for b in range(batch)], axis=0)
        return rows_.reshape((len(xs), batch, hi - lo) + rows_.shape[1:])

    def sample_part(xs, lo):
        return jnp.stack([x[n_prompt:].reshape((n_streams, ds) + x.shape[1:])[:, lo:] for x in xs])

    y_prompt = prompt_part([h], n_meta, t_real)[0]
    y_sample = h[n_prompt:].reshape(n_streams, ds, d)
    return (y_prompt, y_sample,
            prompt_part(k_l, 0, t_real), prompt_part(v_l, 0, t_real), jnp.stack(cp_l),
            sample_part(k_l, 0), sample_part(v_l, 0), jnp.stack(cs_l))
```
